```python
import math
import jax, jax.numpy as jnp
from jax import lax
import numpy as np

D_MODEL = 1024
BATCH = 8
SEQ = 4096
DEPTH = 1

SSD_HEADS = 16
SSD_HEAD_DIM = 64
SSD_GROUPS = 2
SSD_STATE = 128
SSD_CONV = 4
SSD_CHUNK = 128
SSD_INNER = SSD_HEADS * SSD_HEAD_DIM
SSD_XBC = SSD_INNER + 2 * SSD_GROUPS * SSD_STATE
NSA_HEADS = 8
NSA_KV_GROUPS = 2
NSA_HEAD_DIM = 64
NSA_CMP_LEN = 32
NSA_CMP_STRIDE = 16
NSA_CMP_HIDDEN = 256
NSA_SLC_BLOCK = 64
NSA_TOP_N = 16
NSA_WINDOW = 512
NSA_QBLOCK = 64
NSA_INNER = NSA_HEADS * NSA_HEAD_DIM
NSA_KV = 3 * 2 * NSA_KV_GROUPS * NSA_HEAD_DIM
NSA_GATES = 3 * NSA_HEADS
MEM_TOKENS = 256
MEM_HEADS = 4
MEM_HEAD_DIM = 128
MEM_INNER = MEM_HEADS * MEM_HEAD_DIM
N_BRANCH = 3
FFN_HIDDEN = 2816
FFN_CONV = 3
NORM_EPS = 1e-6
NEG = -1e30
BIG = 1e9

IN_SIZES = (SSD_INNER, SSD_XBC, SSD_HEADS, NSA_INNER, NSA_KV, NSA_GATES, MEM_INNER, N_BRANCH * D_MODEL)
IN_WIDTH = sum(IN_SIZES)
IN_OFFSETS = tuple(int(v) for v in np.cumsum(IN_SIZES)[:-1])

kernel_name = "hybrid_ssd_nsa_mem_convffn"


def rmsnorm(x, g):
    xf = x.astype(jnp.float32)
    y = xf * lax.rsqrt(jnp.mean(xf * xf, axis=-1, keepdims=True) + NORM_EPS)
    return (y * g.astype(jnp.float32)).astype(x.dtype)


def causal_dwconv(x, w, b):
    k, c = w.shape
    y = lax.conv_general_dilated(x, w[:, None, :].astype(x.dtype), window_strides=(1,),
                                 padding=[(k - 1, 0)], dimension_numbers=('NWC', 'WIO', 'NWC'),
                                 feature_group_count=c)
    return y + b.astype(x.dtype)


def alibi_slopes(n):
    return jnp.exp2(-8.0 * jnp.arange(1, n + 1, dtype=jnp.float32) / n)


def ssd_mixer(z, xbc_raw, dt_raw, conv_w, conv_b, dt_bias, a_log, d_skip, norm_g):
    bsz, s, _ = z.shape
    f32 = jnp.float32
    g, hg, p, n, L = SSD_GROUPS, SSD_HEADS // SSD_GROUPS, SSD_HEAD_DIM, SSD_STATE, SSD_CHUNK
    nc = s // L
    xbc = jax.nn.silu(causal_dwconv(xbc_raw, conv_w, conv_b))
    xs, bm, cm = jnp.split(xbc, [SSD_INNER, SSD_INNER + g * n], axis=-1)
    xc = xs.reshape(bsz, nc, L, g, hg, p).astype(f32)
    bc = bm.reshape(bsz, nc, L, g, n).astype(f32)
    cc = cm.reshape(bsz, nc, L, g, n).astype(f32)
    dt = jax.nn.softplus(dt_raw.astype(f32) + dt_bias.astype(f32)).reshape(bsz, nc, L, g, hg)
    a = -jnp.exp(a_log.astype(f32)).reshape(g, hg)
    a_cs = jnp.cumsum(jnp.transpose(dt * a, (0, 3, 4, 1, 2)), axis=-1)
    tril = jnp.tril(jnp.ones((L, L), dtype=bool))
    seg = a_cs[..., :, None] - a_cs[..., None, :]
    lmat = jnp.where(tril, jnp.exp(jnp.where(tril, seg, 0.0)), 0.0)
    xdt = xc * dt[..., None]
    cb = jnp.einsum('bclgn,bcsgn->bgcls', cc, bc)
    y_diag = jnp.einsum('bghcls,bcsghp->bclghp', cb[:, :, None] * lmat, xdt)
    decay_states = jnp.exp(a_cs[..., -1:] - a_cs)
    states = jnp.einsum('bclgn,bghcl,bclghp->bcghpn', bc, decay_states, xdt)
    chunk_decay = jnp.exp(a_cs[..., -1])

    def step(h, inp):
        s_c, d_c = inp
        return d_c[..., None, None] * h + s_c, h

    h0 = jnp.zeros((bsz, g, hg, p, n), f32)
    _, prev = lax.scan(step, h0, (jnp.moveaxis(states, 1, 0), jnp.moveaxis(chunk_decay, -1, 0)))
    prev = jnp.moveaxis(prev, 0, 1)
    y_off = jnp.einsum('bclgn,bcghpn,bghcl->bclghp', cc, prev, jnp.exp(a_cs))
    y = y_diag + y_off + xc * d_skip.astype(f32).reshape(g, hg)[..., None]
    y = y.reshape(bsz, s, SSD_INNER).astype(z.dtype)
    return rmsnorm(y * jax.nn.silu(z), norm_g)


def nsa_mixer(q, kv, gates, q_norm, k_norm, cmp_pe, cmp_w1, cmp_w2):
    bsz, s, _ = q.shape
    dtype = q.dtype
    f32 = jnp.float32
    G, hg, dk = NSA_KV_GROUPS, NSA_HEADS // NSA_KV_GROUPS, NSA_HEAD_DIM
    QB, SLC, W = NSA_QBLOCK, NSA_SLC_BLOCK, NSA_WINDOW
    scale = dk ** -0.5
    q = rmsnorm(q.reshape(bsz, s, G, hg, dk), q_norm)
    kv = kv.reshape(bsz, s, 3, 2, G, dk)

    ratio = NSA_CMP_LEN // NSA_CMP_STRIDE
    n_chunk = s // NSA_CMP_STRIDE
    n_cmp = n_chunk - ratio + 1
    chunks = jnp.moveaxis(kv[:, :, 0], 2, 0).reshape(2, bsz, n_chunk, NSA_CMP_STRIDE, G, dk)
    blocks = jnp.concatenate([chunks[:, :, r:r + n_cmp] for r in range(ratio)], axis=3)
    blocks = blocks + cmp_pe[:, None, None, :, None, :].astype(dtype)
    flat = jnp.swapaxes(blocks, 3, 4).reshape(2, bsz, n_cmp, G, NSA_CMP_LEN * dk)
    hid = jax.nn.silu(jnp.einsum('kbngf,kfh->kbngh', flat, cmp_w1))
    cmp = jnp.einsum('kbngh,khd->kbngd', hid, cmp_w2)
    k_cmp = rmsnorm(cmp[0], k_norm[0])
    v_cmp = cmp[1]
    cmp_start = jnp.arange(n_cmp) * NSA_CMP_STRIDE
    cmp_end = cmp_start + NSA_CMP_LEN - 1

    n_slc = s // SLC
    n_top = min(NSA_TOP_N, n_slc)
    k_slc = rmsnorm(kv[:, :, 1, 0], k_norm[1]).reshape(bsz, n_slc, SLC, G, dk).transpose(0, 3, 1, 2, 4)
    v_slc = kv[:, :, 1, 1].reshape(bsz, n_slc, SLC, G, dk).transpose(0, 3, 1, 2, 4)
    slc_start = jnp.arange(n_slc) * SLC
    overlap = jnp.clip(jnp.minimum(cmp_end[:, None], slc_start[None, :] + SLC - 1)
                       - jnp.maximum(cmp_start[:, None], slc_start[None, :]) + 1, 0)
    overlap = overlap.astype(f32) / NSA_CMP_LEN

    pad = jnp.zeros((bsz, W, G, dk), dtype)
    k_win = jnp.concatenate([pad, rmsnorm(kv[:, :, 2, 0], k_norm[2])], axis=1)
    v_win = jnp.concatenate([pad, kv[:, :, 2, 1]], axis=1)

    slopes = alibi_slopes(NSA_HEADS).reshape(G, hg)
    bidx = jnp.arange(bsz)[:, None, None, None]
    gidx = jnp.arange(G)[None, :, None, None]

    def block_fn(args):
        qb, t0 = args
        t = t0 + jnp.arange(QB)
        sc = jnp.einsum('bqghd,bngd->bghqn', qb, k_cmp).astype(f32) * scale
        dist = t[:, None] - cmp_end[None, :]
        sc = sc - slopes[:, :, None, None] * dist.astype(f32)
        valid = dist >= 0
        p_cmp = jnp.where(valid, jax.nn.softmax(jnp.where(valid, sc, NEG), axis=-1), 0.0)
        o_cmp = jnp.einsum('bghqn,bngd->bqghd', p_cmp.astype(dtype), v_cmp)
        imp = jnp.einsum('bghqn,nj->bgqj', p_cmp, overlap)
        jt = (t // SLC)[:, None]
        blk = jnp.arange(n_slc)[None, :]
        forced = (blk == 0) | (blk == jt) | (blk == jt - 1)
        imp = jnp.where(forced, BIG, imp)
        imp = jnp.where(blk > jt, NEG, imp)
        _, idx = lax.top_k(imp, n_top)
        k_sel = k_slc[bidx, gidx, idx]
        v_sel = v_slc[bidx, gidx, idx]
        ss = jnp.einsum('bqghd,bgqkld->bghqkl', qb, k_sel).astype(f32) * scale
        dsel = (t[:, None, None] - (idx[..., None] * SLC + jnp.arange(SLC)))[:, :, None]
        ss = ss - slopes[None, :, :, None, None, None] * dsel.astype(f32)
        ss = jnp.where(dsel >= 0, ss, NEG).reshape(bsz, G, hg, QB, n_top * SLC)
        p_sel = jax.nn.softmax(ss, axis=-1).reshape(bsz, G, hg, QB, n_top, SLC)
        o_slc = jnp.einsum('bghqkl,bgqkld->bqghd', p_sel.astype(dtype), v_sel)
        kw = lax.dynamic_slice_in_dim(k_win, t0, W + QB, axis=1)
        vw = lax.dynamic_slice_in_dim(v_win, t0, W + QB, axis=1)
        sw = jnp.einsum('bqghd,bsgd->bghqs', qb, kw).astype(f32) * scale
        kpos = t0 - W + jnp.arange(W + QB)
        dw = t[:, None] - kpos[None, :]
        vmask = (dw >= 0) & (dw < W) & (kpos[None, :] >= 0)
        sw = sw - slopes[:, :, None, None] * dw.astype(f32)
        p_win = jax.nn.softmax(jnp.where(vmask, sw, NEG), axis=-1)
        o_win = jnp.einsum('bghqs,bsgd->bqghd', p_win.astype(dtype), vw)
        return o_cmp, o_slc, o_win

    nq = s // QB
    q_blocks = jnp.moveaxis(q.reshape(bsz, nq, QB, G, hg, dk), 1, 0)
    o_cmp, o_slc, o_win = lax.map(block_fn, (q_blocks, jnp.arange(nq) * QB))

    def unblock(o):
        return jnp.moveaxis(o, 0, 1).reshape(bsz, s, G, hg, dk)

    gt = jax.nn.sigmoid(gates.astype(f32)).astype(dtype).reshape(bsz, s, 3, G, hg, 1)
    out = gt[:, :, 0] * unblock(o_cmp) + gt[:, :, 1] * unblock(o_slc) + gt[:, :, 2] * unblock(o_win)
    return out.reshape(bsz, s, NSA_INNER)


def memory_attention(q, mem, mem_norm, w_mem_kv, q_norm, k_norm):
    bsz, s, _ = q.shape
    m = mem.shape[1]
    q = rmsnorm(q.reshape(bsz, s, MEM_HEADS, MEM_HEAD_DIM), q_norm)
    kv = (rmsnorm(mem, mem_norm) @ w_mem_kv).reshape(bsz, m, 2, MEM_HEADS, MEM_HEAD_DIM)
    k = rmsnorm(kv[:, :, 0], k_norm)
    v = kv[:, :, 1]
    sc = jnp.einsum('bqhd,bkhd->bhqk', q, k).astype(jnp.float32) * MEM_HEAD_DIM ** -0.5
    p = jax.nn.softmax(sc, axis=-1)
    o = jnp.einsum('bhqk,bkhd->bqhd', p.astype(v.dtype), v)
    return o.reshape(bsz, s, MEM_INNER)


def hybrid_layer(x, mem, norm_mix, w_in, ssd_conv_w, ssd_conv_b, ssd_dt_bias, ssd_a_log, ssd_d,
                 ssd_norm, w_ssd_o, nsa_q_norm, nsa_k_norm, nsa_cmp_pe, nsa_cmp_w1, nsa_cmp_w2,
                 w_nsa_o, mem_norm, w_mem_kv, mem_q_norm, mem_k_norm, w_mem_o, w_out, norm_ffn,
                 w_ffn_up, ffn_conv_w, ffn_conv_b, w_ffn_down):
    bsz, s, d = x.shape
    xn = rmsnorm(x, norm_mix)
    proj = xn @ w_in
    z, xbc, dt_raw, q_nsa, kv_nsa, g_nsa, q_mem, g_merge = jnp.split(proj, IN_OFFSETS, axis=-1)
    y_ssd = ssd_mixer(z, xbc, dt_raw, ssd_conv_w, ssd_conv_b, ssd_dt_bias, ssd_a_log, ssd_d, ssd_norm) @ w_ssd_o
    y_nsa = nsa_mixer(q_nsa, kv_nsa, g_nsa, nsa_q_norm, nsa_k_norm, nsa_cmp_pe, nsa_cmp_w1, nsa_cmp_w2) @ w_nsa_o
    y_mem = memory_attention(q_mem, mem, mem_norm, w_mem_kv, mem_q_norm, mem_k_norm) @ w_mem_o
    g = jax.nn.sigmoid(g_merge.astype(jnp.float32)).astype(x.dtype).reshape(bsz, s, N_BRANCH, d)
    mixed = g[:, :, 0] * y_ssd + g[:, :, 1] * y_nsa + g[:, :, 2] * y_mem
    h = x + mixed @ w_out
    u = causal_dwconv(rmsnorm(h, norm_ffn) @ w_ffn_up, ffn_conv_w, ffn_conv_b)
    gate, val = jnp.split(u, 2, axis=-1)
    return h + (jax.nn.silu(gate) * val) @ w_ffn_down


def setup_inputs(seed: int = 0) -> dict:
    key = jax.random.key(seed)
    ks = jax.random.split(key, 32)
    f32 = jnp.float32
    L = DEPTH

    def nrm(k, shape, fan_in):
        return jax.random.normal(k, shape, f32) * fan_in ** -0.5

    def gain(k, shape):
        return 1.0 + 0.02 * jax.random.normal(k, shape, f32)

    dt = jnp.exp(jax.random.uniform(ks[6], (L, SSD_HEADS), f32, math.log(1e-3), math.log(1e-1)))
    return {
        "x": jax.random.normal(ks[0], (BATCH, SEQ, D_MODEL), f32),
        "mem": jax.random.normal(ks[1], (BATCH, MEM_TOKENS, D_MODEL), f32),
        "norm_mix": gain(ks[2], (L, D_MODEL)),
        "w_in": nrm(ks[3], (L, D_MODEL, IN_WIDTH), D_MODEL),
        "ssd_conv_w": nrm(ks[4], (L, SSD_CONV, SSD_XBC), SSD_CONV),
        "ssd_conv_b": 0.02 * jax.random.normal(ks[5], (L, SSD_XBC), f32),
        "ssd_dt_bias": dt + jnp.log(-jnp.expm1(-dt)),
        "ssd_a_log": jnp.log(jax.random.uniform(ks[7], (L, SSD_HEADS), f32, 1.0, 16.0)),
        "ssd_d": gain(ks[8], (L, SSD_HEADS)),
        "ssd_norm": gain(ks[9], (L, SSD_INNER)),
        "w_ssd_o": nrm(ks[10], (L, SSD_INNER, D_MODEL), SSD_INNER),
        "nsa_q_norm": gain(ks[11], (L, NSA_HEAD_DIM)),
        "nsa_k_norm": gain(ks[12], (L, 3, NSA_HEAD_DIM)),
        "nsa_cmp_pe": 0.02 * jax.random.normal(ks[13], (L, 2, NSA_CMP_LEN, NSA_HEAD_DIM), f32),
        "nsa_cmp_w1": nrm(ks[14], (L, 2, NSA_CMP_LEN * NSA_HEAD_DIM, NSA_CMP_HIDDEN), NSA_CMP_LEN * NSA_HEAD_DIM),
        "nsa_cmp_w2": nrm(ks[15], (L, 2, NSA_CMP_HIDDEN, NSA_HEAD_DIM), NSA_CMP_HIDDEN),
        "w_nsa_o": nrm(ks[16], (L, NSA_INNER, D_MODEL), NSA_INNER),
        "mem_norm": gain(ks[17], (L, D_MODEL)),
        "w_mem_kv": nrm(ks[18], (L, D_MODEL, 2 * MEM_INNER), D_MODEL),
        "mem_q_norm": gain(ks[19], (L, MEM_HEAD_DIM)),
        "mem_k_norm": gain(ks[20], (L, MEM_HEAD_DIM)),
        "w_mem_o": nrm(ks[21], (L, MEM_INNER, D_MODEL), MEM_INNER),
        "w_out": nrm(ks[22], (L, D_MODEL, D_MODEL), D_MODEL),
        "norm_ffn": gain(ks[23], (L, D_MODEL)),
        "w_ffn_up": nrm(ks[24], (L, D_MODEL, 2 * FFN_HIDDEN), D_MODEL),
        "ffn_conv_w": nrm(ks[25], (L, FFN_CONV, 2 * FFN_HIDDEN), FFN_CONV),
        "ffn_conv_b": 0.02 * jax.random.normal(ks[26], (L, 2 * FFN_HIDDEN), f32),
        "w_ffn_down": nrm(ks[27], (L, FFN_HIDDEN, D_MODEL), FFN_HIDDEN),
    }


def reference(x, mem, norm_mix, w_in, ssd_conv_w, ssd_conv_b, ssd_dt_bias, ssd_a_log, ssd_d,
              ssd_norm, w_ssd_o, nsa_q_norm, nsa_k_norm, nsa_cmp_pe, nsa_cmp_w1, nsa_cmp_w2,
              w_nsa_o, mem_norm, w_mem_kv, mem_q_norm, mem_k_norm, w_mem_o, w_out, norm_ffn,
              w_ffn_up, ffn_conv_w, ffn_conv_b, w_ffn_down):
    h = x
    for i in range(DEPTH):
        h = hybrid_layer(h, mem, norm_mix[i], w_in[i], ssd_conv_w[i], ssd_conv_b[i], ssd_dt_bias[i],
                         ssd_a_log[i], ssd_d[i], ssd_norm[i], w_ssd_o[i], nsa_q_norm[i], nsa_k_norm[i],
                         nsa_cmp_pe[i], nsa_cmp_w1[i], nsa_cmp_w2[i], w_nsa_o[i], mem_norm[i],
                         w_mem_kv[i], mem_q_norm[i], mem_k_norm[i], w_mem_o[i], w_out[i], norm_ffn[i],
                         w_ffn_up[i], ffn_conv_w[i], ffn_conv_b[i], w_ffn_down[i])
    return h
```

```python
import functools

import jax
import jax.numpy as jnp
from jax import lax
from jax.experimental import pallas as pl
from jax.experimental.pallas import tpu as pltpu

F32 = jnp.float32
BF16 = jnp.bfloat16

D_MODEL = 1024
SSD_HEADS = 16
SSD_HEAD_DIM = 64
SSD_GROUPS = 2
SSD_STATE = 128
SSD_CONV = 4
SSD_CHUNK = 128
SSD_INNER = SSD_HEADS * SSD_HEAD_DIM
SSD_XBC = SSD_INNER + 2 * SSD_GROUPS * SSD_STATE
NSA_HEADS = 8
NSA_KV_GROUPS = 2
NSA_HG = NSA_HEADS // NSA_KV_GROUPS
NSA_HEAD_DIM = 64
NSA_CMP_LEN = 32
NSA_CMP_STRIDE = 16
NSA_CMP_HIDDEN = 256
NSA_SLC_BLOCK = 64
NSA_TOP_N = 16
NSA_WINDOW = 512
NSA_INNER = NSA_HEADS * NSA_HEAD_DIM
NSA_KV = 3 * 2 * NSA_KV_GROUPS * NSA_HEAD_DIM
NSA_GATES = 3 * NSA_HEADS
MEM_HEADS = 4
MEM_HEAD_DIM = 128
MEM_INNER = MEM_HEADS * MEM_HEAD_DIM
N_BRANCH = 3
FFN_HIDDEN = 2816
FFN_CONV = 3
NORM_EPS = 1e-6
NEG = -1e30
BIG = 1e9

IN_SIZES = (SSD_INNER, SSD_XBC, SSD_HEADS, NSA_INNER, NSA_KV, NSA_GATES, MEM_INNER, N_BRANCH * D_MODEL)

LANES = 128
BF16_SUBLANES = 16
VMEM_LIMIT = 56 * 1024 * 1024

ROW_TILE = 256
NSA_TQ = 128
NSA_TK = 512
FFN_CHUNK = 256


def _dot(a, b):
    return jnp.dot(a, b, preferred_element_type=F32)


def _dot_nt(a, b):
    return lax.dot_general(a, b, (((1,), (1,)), ((), ())), preferred_element_type=F32)


def _split3(x):
    hi = x.astype(BF16)
    r1 = x - hi.astype(F32)
    mid = r1.astype(BF16)
    lo = (r1 - mid.astype(F32)).astype(BF16)
    return hi, mid, lo


def _dot3(x, w):
    hi, mid, lo = _split3(x)
    return _dot(hi, w) + _dot(mid, w) + _dot(lo, w)


def _silu(x):
    return x * jax.nn.sigmoid(x)


def _rms_rows(x, gain):
    return x * lax.rsqrt(jnp.mean(x * x, axis=-1, keepdims=True) + NORM_EPS) * gain


def _const_spec(shape):
    nd = len(shape)
    return pl.BlockSpec(shape, lambda *_: (0,) * nd, pipeline_mode=pl.Buffered(1))


def _params(sem):
    return pltpu.CompilerParams(dimension_semantics=sem, vmem_limit_bytes=VMEM_LIMIT)


def _head_rms(v, gain, d):
    if d == LANES:
        ss = jnp.sum(v * v, axis=-1, keepdims=True)
        return v * lax.rsqrt(ss / d + NORM_EPS) * gain
    lane = lax.broadcasted_iota(jnp.int32, v.shape, 1)
    head = lane < d
    ss = jnp.sum(jnp.where(head, v * v, 0.0), axis=-1, keepdims=True)
    return jnp.where(head, v * lax.rsqrt(ss / d + NORM_EPS) * gain, v)


def _in_proj_kernel(x_ref, g_ref, wz, wxbc, wdt, wq, wkvc, wkv, wgn, wqm, wgm, qn_ref, kn_ref, mqn_ref,
                    z_o, xbc_o, dt_o, q_o, kvc_o, kv_o, gn_o, qm_o, gm_o):
    xn = _rms_rows(x_ref[...], g_ref[...]).astype(BF16)
    z_o[...] = _dot(xn, wz[...]).astype(z_o.dtype)
    xbc_o[...] = _dot(xn, wxbc[...]).astype(xbc_o.dtype)
    dt_o[...] = _dot(xn, wdt[...])
    gn_o[...] = _dot(xn, wgn[...])
    gm_o[...] = _dot(xn, wgm[...]).astype(gm_o.dtype)
    kvc_o[...] = _dot(xn, wkvc[...]).astype(kvc_o.dtype)
    q = _dot(xn, wq[...])
    for h in range(NSA_HEADS):
        sl = slice(h * LANES, (h + 1) * LANES)
        q_o[:, sl] = _head_rms(q[:, sl], qn_ref[...], NSA_HEAD_DIM).astype(q_o.dtype)
    kv = _dot(xn, wkv[...])
    for j in range(2 * NSA_KV_GROUPS):
        sl = slice(j * LANES, (j + 1) * LANES)
        kv_o[:, sl] = _head_rms(kv[:, sl], kn_ref[j % 2:j % 2 + 1, :], NSA_HEAD_DIM).astype(kv_o.dtype)
    qm = _dot(xn, wqm[...])
    for h in range(MEM_HEADS):
        sl = slice(h * LANES, (h + 1) * LANES)
        qm_o[:, sl] = _head_rms(qm[:, sl], mqn_ref[...], MEM_HEAD_DIM).astype(qm_o.dtype)


def _in_proj(x2, norm_mix, w_in, nsa_q_norm, nsa_k_norm, mem_q_norm):
    T = x2.shape[0]
    TM = ROW_TILE
    o = [0]
    for s in IN_SIZES:
        o.append(o[-1] + s)
    wz = w_in[:, o[0]:o[1]]
    wxbc = w_in[:, o[1]:o[2]]
    wdt = jnp.pad(w_in[:, o[2]:o[3]], ((0, 0), (0, LANES - SSD_HEADS)))
    wq = w_in[:, o[3]:o[4]].reshape(D_MODEL, NSA_HEADS, NSA_HEAD_DIM)
    wq = jnp.pad(wq, ((0, 0), (0, 0), (0, LANES - NSA_HEAD_DIM))).reshape(D_MODEL, NSA_HEADS * LANES)
    wkv5 = w_in[:, o[4]:o[5]].reshape(D_MODEL, 3, 2, NSA_KV_GROUPS, NSA_HEAD_DIM)
    wkvc = wkv5[:, 0].transpose(0, 2, 1, 3).reshape(D_MODEL, NSA_KV_GROUPS * LANES)
    wkv = wkv5[:, 1:].transpose(0, 3, 1, 2, 4).reshape(D_MODEL, NSA_KV_GROUPS * 2 * LANES)
    wgn = w_in[:, o[5]:o[6]].reshape(D_MODEL, 3, NSA_KV_GROUPS, NSA_HG).transpose(0, 2, 1, 3)
    wgn = jnp.pad(wgn.reshape(D_MODEL, NSA_KV_GROUPS, 3 * NSA_HG), ((0, 0), (0, 0), (0, LANES - 3 * NSA_HG)))
    wgn = wgn.reshape(D_MODEL, NSA_KV_GROUPS * LANES)
    wqm = w_in[:, o[6]:o[7]]
    wgm = w_in[:, o[7]:o[8]]
    weights = [w.astype(BF16) for w in (wz, wxbc, wdt, wq, wkvc, wkv, wgn, wqm, wgm)]
    qn = jnp.pad(nsa_q_norm * (NSA_HEAD_DIM ** -0.5), (0, LANES - NSA_HEAD_DIM)).reshape(1, LANES)
    kn = jnp.concatenate([nsa_k_norm[1:3], jnp.ones((2, LANES - NSA_HEAD_DIM), F32)], axis=1)
    mqn = (mem_q_norm * (MEM_HEAD_DIM ** -0.5)).reshape(1, LANES)
    widths = [w.shape[1] for w in weights]
    dtypes = [BF16, BF16, F32, BF16, BF16, BF16, F32, BF16, BF16]
    out_shape = [jax.ShapeDtypeStruct((T, w), dt) for w, dt in zip(widths, dtypes)]
    row = lambda w: pl.BlockSpec((TM, w), lambda i: (i, 0))
    return pl.pallas_call(
        _in_proj_kernel,
        grid=(T // TM,),
        in_specs=[row(D_MODEL), _const_spec((1, D_MODEL))] + [_const_spec(w.shape) for w in weights]
        + [_const_spec((1, LANES)), _const_spec((2, LANES)), _const_spec((1, LANES))],
        out_specs=[row(w) for w in widths],
        out_shape=out_shape,
        compiler_params=_params(("parallel",)),
        name="in_proj",
    )(x2, norm_mix.reshape(1, D_MODEL), *weights, qn, kn, mqn)


def _softplus(x):
    return jnp.maximum(x, 0.0) + jnp.log1p(jnp.exp(-jnp.abs(x)))


def _ssd_kernel(z_ref, xbc_ref, halo_ref, dt_ref, cw_ref, cb_ref, dtb_ref, alog_ref, dexp_ref, ng_ref, r_ref,
                o_ref, state, ext):
    L = SSD_CHUNK
    HALO = BF16_SUBLANES
    c = pl.program_id(1)

    @pl.when(c == 0)
    def _():
        state[...] = jnp.zeros_like(state)

    halo = halo_ref[...].astype(F32) * jnp.where(c == 0, 0.0, 1.0)
    ext[0:HALO, :] = halo
    ext[HALO:HALO + L, :] = xbc_ref[...].astype(F32)
    acc = jnp.broadcast_to(cb_ref[...], (L, SSD_XBC))
    for k in range(SSD_CONV):
        acc = acc + cw_ref[k:k + 1, :] * ext[pl.ds(HALO - (SSD_CONV - 1) + k, L), :]
    xa = _silu(acc)
    xs = xa[:, :SSD_INNER]

    lane = lax.broadcasted_iota(jnp.int32, (L, LANES), 1)
    rowi = lax.broadcasted_iota(jnp.int32, (L, L), 0)
    coli = lax.broadcasted_iota(jnp.int32, (L, L), 1)
    tril = rowi >= coli
    tril_w = jnp.where(tril, 1.0, 0.0).astype(BF16)

    head_lane = lane < SSD_HEADS
    dt = jnp.where(head_lane, _softplus(dt_ref[...] + dtb_ref[...]), 0.0)
    d_a = dt * jnp.where(head_lane[0:1], -jnp.exp(alog_ref[...]), 0.0)
    cs = sum(_dot(tril_w, part) for part in _split3(d_a))
    cs_t = cs.T
    cs_last = cs[L - 1:L, :]
    ecs = jnp.exp(cs)
    decay = jnp.exp(cs_last - cs)
    r = r_ref[...]
    dt_x = _dot3(dt, r)
    ecs_x = _dot3(ecs, r)
    decay_x = _dot3(decay, r)

    xdt = xs * dt_x
    xdtd = (xdt * decay_x).astype(BF16)
    xdt16 = xdt.astype(BF16)
    y_skip = xs * dexp_ref[...]
    lane_lo = lane < SSD_HEAD_DIM

    hpg = SSD_HEADS // SSD_GROUPS
    gw = hpg * SSD_HEAD_DIM
    y_blocks = []
    for g in range(SSD_GROUPS):
        b_g = xa[:, SSD_INNER + g * SSD_STATE:SSD_INNER + (g + 1) * SSD_STATE]
        c_g = xa[:, SSD_INNER + (SSD_GROUPS + g) * SSD_STATE:SSD_INNER + (SSD_GROUPS + g + 1) * SSD_STATE]
        b16 = b_g.astype(BF16)
        c16 = c_g.astype(BF16)
        cb = _dot_nt(c16, b16)
        st = state[:, g * gw:(g + 1) * gw]
        y_off = _dot(c16, st.astype(BF16)) * ecs_x[:, g * gw:(g + 1) * gw]
        bt16 = b_g.T.astype(BF16)
        state[:, g * gw:(g + 1) * gw] = (st * ecs_x[L - 1:L, g * gw:(g + 1) * gw]
                                         + _dot(bt16, xdtd[:, g * gw:(g + 1) * gw]))
        for hp in range(hpg // 2):
            col = g * gw + hp * LANES
            xp = xdt16[:, col:col + LANES]
            pair = []
            for j in range(2):
                h = g * hpg + hp * 2 + j
                seg = cs[:, h:h + 1] - cs_t[h:h + 1, :]
                lm = jnp.where(tril, jnp.exp(jnp.where(tril, seg, 0.0)), 0.0)
                pair.append(_dot((cb * lm).astype(BF16), xp))
            y_diag = jnp.where(lane_lo, pair[0], pair[1])
            y_blocks.append(y_diag + y_off[:, hp * LANES:(hp + 1) * LANES] + y_skip[:, col:col + LANES])
    y = jnp.concatenate(y_blocks, axis=1)
    yz = y * _silu(z_ref[...].astype(F32))
    o_ref[...] = _rms_rows(yz, ng_ref[...]).astype(o_ref.dtype)


def _ssd(z, xbc, dt, conv_w, conv_b, dt_bias, a_log, d_skip, norm_g):
    B, S, _ = z.shape
    L = SSD_CHUNK
    HALO = BF16_SUBLANES
    pad = lambda v: jnp.pad(v, (0, LANES - SSD_HEADS)).reshape(1, LANES)
    d_exp = jnp.repeat(d_skip, SSD_HEAD_DIM).reshape(1, SSD_INNER)
    expand = (jnp.arange(LANES)[:, None] == (jnp.arange(SSD_INNER)[None, :] // SSD_HEAD_DIM)).astype(BF16)
    blk = lambda w: pl.BlockSpec((None, L, w), lambda b, c: (b, c, 0))
    halo_spec = pl.BlockSpec((None, HALO, SSD_XBC), lambda b, c: (b, jnp.maximum(c * (L // HALO) - 1, 0), 0))
    return pl.pallas_call(
        _ssd_kernel,
        grid=(B, S // L),
        in_specs=[blk(SSD_INNER), blk(SSD_XBC), halo_spec, blk(LANES),
                  _const_spec((SSD_CONV, SSD_XBC)), _const_spec((1, SSD_XBC)), _const_spec((1, LANES)),
                  _const_spec((1, LANES)), _const_spec((1, SSD_INNER)), _const_spec((1, SSD_INNER)),
                  _const_spec((LANES, SSD_INNER))],
        out_specs=blk(SSD_INNER),
        out_shape=jax.ShapeDtypeStruct((B, S, SSD_INNER), BF16),
        scratch_shapes=[pltpu.VMEM((SSD_STATE, SSD_INNER), F32), pltpu.VMEM((HALO + L, SSD_XBC), F32)],
        compiler_params=_params(("parallel", "arbitrary")),
        name="ssd",
    )(z, xbc, xbc, dt, conv_w, conv_b.reshape(1, SSD_XBC), pad(dt_bias), pad(a_log), d_exp,
      norm_g.reshape(1, SSD_INNER), expand)


def _cmp_kernel(kvc_ref, pea_ref, peb_ref, w1a_ref, w1b_ref, w2_ref, kn_ref, o_ref):
    n_chunk = kvc_ref.shape[0]
    per = NSA_CMP_STRIDE
    row = lax.broadcasted_iota(jnp.int32, (n_chunk, LANES), 0)
    for g in range(NSA_KV_GROUPS):
        x = jnp.concatenate([kvc_ref[:, (NSA_KV_GROUPS * l + g) * LANES:(NSA_KV_GROUPS * l + g + 1) * LANES]
                             for l in range(per)], axis=1).astype(F32)
        a = _dot((x + pea_ref[...]).astype(BF16), w1a_ref[...])
        b = _dot((x + peb_ref[...]).astype(BF16), w1b_ref[...])
        hid = _silu(a + pltpu.roll(b, n_chunk - 1, axis=0))
        cmp = _dot(hid.astype(BF16), w2_ref[...])
        cmp = _head_rms(cmp, kn_ref[...], NSA_HEAD_DIM)
        o_ref[g] = jnp.where(row < n_chunk - 1, cmp, 0.0).astype(o_ref.dtype)


def _cmp(kvc, nsa_cmp_pe, nsa_cmp_w1, nsa_cmp_w2, k_norm0):
    B, S, _ = kvc.shape
    n_chunk = S // NSA_CMP_STRIDE
    per = NSA_CMP_STRIDE
    dk = NSA_HEAD_DIM
    kvc_r = kvc.reshape(B, n_chunk, per * NSA_KV_GROUPS * LANES)
    pe = jnp.concatenate([nsa_cmp_pe[0], nsa_cmp_pe[1]], axis=1)
    pea = pe[:per].reshape(1, per * LANES)
    peb = pe[per:].reshape(1, per * LANES)
    w1 = nsa_cmp_w1.reshape(2, NSA_CMP_LEN, dk, NSA_CMP_HIDDEN)
    zero = jnp.zeros((NSA_CMP_LEN, dk, NSA_CMP_HIDDEN), F32)
    w1bd = jnp.concatenate([jnp.concatenate([w1[0], zero], axis=2), jnp.concatenate([zero, w1[1]], axis=2)], axis=1)
    w1a = w1bd[:per].reshape(per * LANES, 2 * NSA_CMP_HIDDEN).astype(BF16)
    w1b = w1bd[per:].reshape(per * LANES, 2 * NSA_CMP_HIDDEN).astype(BF16)
    z2 = jnp.zeros((NSA_CMP_HIDDEN, dk), F32)
    w2bd = jnp.concatenate([jnp.concatenate([nsa_cmp_w2[0], z2], axis=1),
                            jnp.concatenate([z2, nsa_cmp_w2[1]], axis=1)], axis=0).astype(BF16)
    kn = jnp.concatenate([k_norm0, jnp.ones((LANES - dk,), F32)]).reshape(1, LANES)
    return pl.pallas_call(
        _cmp_kernel,
        grid=(B,),
        in_specs=[pl.BlockSpec((None, n_chunk, per * NSA_KV_GROUPS * LANES), lambda b: (b, 0, 0)),
                  _const_spec(pea.shape), _const_spec(peb.shape), _const_spec(w1a.shape), _const_spec(w1b.shape),
                  _const_spec(w2bd.shape), _const_spec(kn.shape)],
        out_specs=pl.BlockSpec((None, NSA_KV_GROUPS, n_chunk, LANES), lambda b: (b, 0, 0, 0)),
        out_shape=jax.ShapeDtypeStruct((B, NSA_KV_GROUPS, n_chunk, LANES), BF16),
        compiler_params=_params(("parallel",)),
        name="nsa_cmp",
    )(kvc_r, pea, peb, w1a, w1b, w2bd, kn)


def _nsa_kernel(q_ref, kvc_ref, kvs_ref, kvw_ref, gate_ref, ov_ref, e_ref, o_ref, *, seq):
    TQ, TK, W = NSA_TQ, NSA_TK, NSA_WINDOW
    HG = NSA_HG
    n_cmp_pad = kvc_ref.shape[0]
    n_slc = seq // NSA_SLC_BLOCK
    g = pl.program_id(1)
    t0 = pl.program_id(2) * TQ
    slope_g = jnp.where(g == 0, 1.0, 2.0 ** (-HG)).astype(F32)
    slopes = [slope_g * (2.0 ** -(h + 1)) for h in range(HG)]

    q = q_ref[...]
    qs = jnp.concatenate([q[:, h * LANES:(h + 1) * LANES] for h in range(HG)], axis=0)
    tq = t0 + lax.broadcasted_iota(jnp.int32, (TQ, 1), 0)

    kvc = kvc_ref[...]
    sc = _dot_nt(qs, kvc)
    cend = lax.broadcasted_iota(jnp.int32, (1, n_cmp_pad), 1) * NSA_CMP_STRIDE + (NSA_CMP_LEN - 1)
    dist = tq - cend
    valid = dist >= 0
    distf = dist.astype(F32)
    psum = jnp.zeros((TQ, n_cmp_pad), F32)
    p_list = []
    for h in range(HG):
        s = jnp.where(valid, sc[h * TQ:(h + 1) * TQ] - slopes[h] * distf, NEG)
        m = jnp.max(s, axis=-1, keepdims=True)
        p = jnp.where(valid, jnp.exp(s - m), 0.0)
        l = jnp.sum(p, axis=-1, keepdims=True)
        p = p / jnp.where(l > 0.0, l, 1.0)
        psum = psum + p
        p_list.append(p.astype(BF16))
    o_cmp = _dot(jnp.concatenate(p_list, axis=0), kvc)

    imp = sum(_dot_nt(ov_ref[...], part) for part in _split3(psum))
    imp = imp[0:n_slc]
    blk = lax.broadcasted_iota(jnp.int32, (n_slc, 1), 0)
    jt = (t0 + lax.broadcasted_iota(jnp.int32, (1, TQ), 1)) // NSA_SLC_BLOCK
    blk_q = jnp.broadcast_to(blk, (n_slc, TQ))
    imp = jnp.where(blk_q == 0, BIG, imp)
    imp = jnp.where(blk_q == jt, BIG, imp)
    imp = jnp.where(blk_q == jt - 1, BIG, imp)
    imp = jnp.where(blk_q > jt, NEG, imp)
    SUB = 8
    row8 = lax.broadcasted_iota(jnp.int32, (SUB, 1), 0)
    groups = [imp[k * SUB:(k + 1) * SUB] for k in range(n_slc // SUB)]
    ranks = [jnp.zeros((SUB, TQ), F32) for _ in groups]
    for i in range(n_slc):
        vi = jnp.broadcast_to(imp[i:i + 1, :], (SUB, TQ))
        for k, grp in enumerate(groups):
            if k * SUB > i:
                ahead = jnp.where(vi >= grp, 1.0, 0.0)
            elif k * SUB + SUB - 1 < i:
                ahead = jnp.where(vi > grp, 1.0, 0.0)
            else:
                ahead = jnp.where(row8 + k * SUB > i, jnp.where(vi >= grp, 1.0, 0.0), jnp.where(vi > grp, 1.0, 0.0))
            ranks[k] = ranks[k] + ahead
    rank = jnp.concatenate(ranks, axis=0)
    sel = jnp.where(rank < NSA_TOP_N, jnp.where(blk_q <= jt, 1.0, 0.0), 0.0)
    if n_slc < LANES:
        sel = jnp.concatenate([sel, jnp.zeros((LANES - n_slc, TQ), F32)], axis=0)
    sel_q = sel.T.astype(BF16)

    def slc_step(kt, carry):
        m, l, acc = carry
        k0 = pl.multiple_of(kt * TK, TK)
        kv = kvs_ref[pl.ds(k0, TK), :]
        s = _dot_nt(qs, kv)
        chosen = _dot(sel_q, e_ref[:, pl.ds(k0, TK)])
        kpos = k0 + lax.broadcasted_iota(jnp.int32, (1, TK), 1)
        ok = (chosen > 0.5) & (kpos <= tq)
        kposf = (kpos - t0).astype(F32)
        s = jnp.concatenate([jnp.where(ok, s[h * TQ:(h + 1) * TQ] + slopes[h] * kposf, NEG)
                             for h in range(HG)], axis=0)
        m_new = jnp.maximum(m, jnp.max(s, axis=-1, keepdims=True))
        alpha = jnp.exp(m - m_new)
        p = jnp.exp(s - m_new)
        l = alpha * l + jnp.sum(p, axis=-1, keepdims=True)
        acc = alpha * acc + _dot(p.astype(BF16), kv)
        return m_new, l, acc

    n_kt = (t0 + TQ + TK - 1) // TK
    init = (jnp.full((HG * TQ, 1), NEG, F32), jnp.zeros((HG * TQ, 1), F32), jnp.zeros((HG * TQ, LANES), F32))
    _, l_s, acc_s = lax.fori_loop(0, n_kt, slc_step, init)
    o_slc = acc_s / l_s

    band = W + TQ
    w0 = pl.multiple_of(jnp.maximum(t0 - W, 0), TQ)
    kvw = kvw_ref[pl.ds(w0, band), :]
    sw = _dot_nt(qs, kvw)
    kpos = w0 + lax.broadcasted_iota(jnp.int32, (1, band), 1)
    dw = tq - kpos
    okw = (dw >= 0) & (dw < W)
    kposf = (kpos - t0).astype(F32)
    pw_list = []
    lw_list = []
    for h in range(HG):
        s = jnp.where(okw, sw[h * TQ:(h + 1) * TQ] + slopes[h] * kposf, NEG)
        m = jnp.max(s, axis=-1, keepdims=True)
        p = jnp.exp(s - m)
        lw_list.append(jnp.sum(p, axis=-1, keepdims=True))
        pw_list.append(p.astype(BF16))
    o_win = _dot(jnp.concatenate(pw_list, axis=0), kvw) / jnp.concatenate(lw_list, axis=0)

    gt = jax.nn.sigmoid(gate_ref[...])
    lane = lax.broadcasted_iota(jnp.int32, (TQ, LANES), 1)
    comb = []
    for h in range(HG):
        rows = slice(h * TQ, (h + 1) * TQ)
        comb.append(gt[:, h:h + 1] * o_cmp[rows] + gt[:, HG + h:HG + h + 1] * o_slc[rows]
                    + gt[:, 2 * HG + h:2 * HG + h + 1] * o_win[rows])
    out = []
    for hp in range(HG // 2):
        out.append(jnp.where(lane < NSA_HEAD_DIM, pltpu.roll(comb[2 * hp], NSA_HEAD_DIM, axis=1), comb[2 * hp + 1]))
    o_ref[...] = jnp.concatenate(out, axis=1).astype(o_ref.dtype)


def _nsa(q, kvcmp, kv, gates, B, S):
    TQ = NSA_TQ
    G = NSA_KV_GROUPS
    n_cmp_pad = S // NSA_CMP_STRIDE
    n_slc = S // NSA_SLC_BLOCK
    cs = jnp.arange(n_cmp_pad) * NSA_CMP_STRIDE
    ce = cs + NSA_CMP_LEN - 1
    ss = jnp.arange(LANES) * NSA_SLC_BLOCK
    overlap = jnp.clip(jnp.minimum(ce[None, :], ss[:, None] + NSA_SLC_BLOCK - 1)
                       - jnp.maximum(cs[None, :], ss[:, None]) + 1, 0).astype(F32) / NSA_CMP_LEN
    valid_pair = (jnp.arange(LANES)[:, None] < n_slc) & (jnp.arange(n_cmp_pad)[None, :] < n_cmp_pad - 1)
    ov_t = jnp.where(valid_pair, overlap, 0.0).astype(BF16)
    expand = (jnp.arange(LANES)[:, None] == (jnp.arange(S)[None, :] // NSA_SLC_BLOCK)).astype(BF16)
    q3 = q.reshape(B, S, NSA_HEADS * LANES)
    kv3 = kv.reshape(B, S, G * 2 * LANES)
    g3 = gates.reshape(B, S, G * LANES)
    return pl.pallas_call(
        functools.partial(_nsa_kernel, seq=S),
        grid=(B, G, S // TQ),
        in_specs=[pl.BlockSpec((None, TQ, NSA_HG * LANES), lambda b, g, i: (b, i, g)),
                  pl.BlockSpec((None, None, n_cmp_pad, LANES), lambda b, g, i: (b, g, 0, 0)),
                  pl.BlockSpec((None, S, LANES), lambda b, g, i: (b, 0, 2 * g)),
                  pl.BlockSpec((None, S, LANES), lambda b, g, i: (b, 0, 2 * g + 1)),
                  pl.BlockSpec((None, TQ, LANES), lambda b, g, i: (b, i, g)),
                  _const_spec(ov_t.shape), _const_spec(expand.shape)],
        out_specs=pl.BlockSpec((None, TQ, NSA_HG * NSA_HEAD_DIM), lambda b, g, i: (b, i, g)),
        out_shape=jax.ShapeDtypeStruct((B, S, NSA_INNER), BF16),
        compiler_params=_params(("parallel", "parallel", "arbitrary")),
        name="nsa_attn",
    )(q3, kvcmp, kv3, kv3, g3, ov_t, expand)


def _mem_kv_kernel(mem_ref, g_ref, w_ref, kn_ref, k_o, v_o):
    mn = _rms_rows(mem_ref[...], g_ref[...]).astype(BF16)
    kv = _dot(mn, w_ref[...])
    for h in range(MEM_HEADS):
        sl = slice(h * LANES, (h + 1) * LANES)
        k_o[:, sl] = _head_rms(kv[:, sl], kn_ref[...], MEM_HEAD_DIM).astype(k_o.dtype)
    v_o[...] = kv[:, MEM_INNER:].astype(v_o.dtype)


def _mem_kv(mem, mem_norm, w_mem_kv, mem_k_norm):
    B, M, _ = mem.shape
    spec_o = pl.BlockSpec((None, M, MEM_INNER), lambda b: (b, 0, 0))
    return pl.pallas_call(
        _mem_kv_kernel,
        grid=(B,),
        in_specs=[pl.BlockSpec((None, M, D_MODEL), lambda b: (b, 0, 0)), _const_spec((1, D_MODEL)),
                  _const_spec((D_MODEL, 2 * MEM_INNER)), _const_spec((1, LANES))],
        out_specs=[spec_o, spec_o],
        out_shape=[jax.ShapeDtypeStruct((B, M, MEM_INNER), BF16)] * 2,
        compiler_params=_params(("parallel",)),
        name="mem_kv",
    )(mem, mem_norm.reshape(1, D_MODEL), w_mem_kv.astype(BF16), mem_k_norm.reshape(1, LANES))


def _merge_kernel(x_ref, ssd_ref, nsa_ref, qm_ref, gm_ref, km_ref, vm_ref, wso, wno, wmo, wout, h_o):
    o_mem = []
    for h in range(MEM_HEADS):
        sl = slice(h * LANES, (h + 1) * LANES)
        s = _dot_nt(qm_ref[:, sl], km_ref[:, sl])
        p = jnp.exp(s - jnp.max(s, axis=-1, keepdims=True))
        o = _dot(p.astype(BF16), vm_ref[:, sl]) / jnp.sum(p, axis=-1, keepdims=True)
        o_mem.append(o.astype(BF16))
    y_mem = _dot(jnp.concatenate(o_mem, axis=1), wmo[...])
    y_ssd = _dot(ssd_ref[...], wso[...])
    y_nsa = _dot(nsa_ref[...], wno[...])
    gate = lambda j: jax.nn.sigmoid(gm_ref[:, j * D_MODEL:(j + 1) * D_MODEL].astype(F32))
    mixed = gate(0) * y_ssd + gate(1) * y_nsa + gate(2) * y_mem
    h_o[...] = x_ref[...] + _dot(mixed.astype(BF16), wout[...])


def _merge(x2, ssd_n, nsa_o, q_mem, g_merge, k_mem, v_mem, w_ssd_o, w_nsa_o, w_mem_o, w_out, S):
    T = x2.shape[0]
    TM = ROW_TILE
    M = k_mem.shape[1]
    per_batch = S // TM
    row = lambda w: pl.BlockSpec((TM, w), lambda i: (i, 0))
    mem_spec = pl.BlockSpec((None, M, MEM_INNER), lambda i: (i // per_batch, 0, 0))
    weights = [w.astype(BF16) for w in (w_ssd_o, w_nsa_o, w_mem_o, w_out)]
    return pl.pallas_call(
        _merge_kernel,
        grid=(T // TM,),
        in_specs=[row(D_MODEL), row(SSD_INNER), row(NSA_INNER), row(MEM_INNER), row(N_BRANCH * D_MODEL),
                  mem_spec, mem_spec] + [_const_spec(w.shape) for w in weights],
        out_specs=row(D_MODEL),
        out_shape=jax.ShapeDtypeStruct((T, D_MODEL), F32),
        compiler_params=_params(("parallel",)),
        name="merge",
    )(x2, ssd_n, nsa_o, q_mem, g_merge, k_mem, v_mem, *weights)


def _ffn_kernel(h_ref, halo_ref, g_ref, wup, cw_ref, cb_ref, wdown, o_ref, ext_g, ext_v, *, per_batch):
    TM = h_ref.shape[0]
    HALO = halo_ref.shape[0]
    CW = FFN_CHUNK
    first = pl.program_id(0) % per_batch == 0
    h = h_ref[...]
    halo_n = _rms_rows(halo_ref[...], g_ref[...]) * jnp.where(first, 0.0, 1.0)
    hn = jnp.concatenate([halo_n, _rms_rows(h, g_ref[...])], axis=0).astype(BF16)
    acc = jnp.zeros((TM, D_MODEL), F32)
    for c in range(FFN_HIDDEN // CW):
        conv = []
        for half, ext in ((0, ext_g), (1, ext_v)):
            col = half * FFN_HIDDEN + c * CW
            ext[...] = _dot(hn, wup[:, col:col + CW])
            u = jnp.broadcast_to(cb_ref[:, col:col + CW], (TM, CW))
            for k in range(FFN_CONV):
                u = u + cw_ref[k:k + 1, col:col + CW] * ext[pl.ds(HALO - (FFN_CONV - 1) + k, TM), :]
            conv.append(u)
        act = (_silu(conv[0]) * conv[1]).astype(BF16)
        acc = acc + _dot(act, wdown[c * CW:(c + 1) * CW, :])
    o_ref[...] = h + acc


def _ffn(h2, norm_ffn, w_ffn_up, ffn_conv_w, ffn_conv_b, w_ffn_down, S):
    T = h2.shape[0]
    TM = ROW_TILE
    HALO = 8
    per_batch = S // TM
    return pl.pallas_call(
        functools.partial(_ffn_kernel, per_batch=per_batch),
        grid=(T // TM,),
        in_specs=[pl.BlockSpec((TM, D_MODEL), lambda i: (i, 0)),
                  pl.BlockSpec((HALO, D_MODEL), lambda i: (jnp.maximum(i * (TM // HALO) - 1, 0), 0)),
                  _const_spec((1, D_MODEL)), _const_spec((D_MODEL, 2 * FFN_HIDDEN)),
                  _const_spec((FFN_CONV, 2 * FFN_HIDDEN)), _const_spec((1, 2 * FFN_HIDDEN)),
                  _const_spec((FFN_HIDDEN, D_MODEL))],
        out_specs=pl.BlockSpec((TM, D_MODEL), lambda i: (i, 0)),
        out_shape=jax.ShapeDtypeStruct((T, D_MODEL), F32),
        scratch_shapes=[pltpu.VMEM((HALO + TM, FFN_CHUNK), F32), pltpu.VMEM((HALO + TM, FFN_CHUNK), F32)],
        compiler_params=_params(("parallel",)),
        name="ffn",
    )(h2, h2, norm_ffn.reshape(1, D_MODEL), w_ffn_up.astype(BF16), ffn_conv_w,
      ffn_conv_b.reshape(1, 2 * FFN_HIDDEN), w_ffn_down.astype(BF16))


def _layer(x, mem, norm_mix, w_in, ssd_conv_w, ssd_conv_b, ssd_dt_bias, ssd_a_log, ssd_d, ssd_norm, w_ssd_o,
           nsa_q_norm, nsa_k_norm, nsa_cmp_pe, nsa_cmp_w1, nsa_cmp_w2, w_nsa_o, mem_norm, w_mem_kv,
           mem_q_norm, mem_k_norm, w_mem_o, w_out, norm_ffn, w_ffn_up, ffn_conv_w, ffn_conv_b, w_ffn_down):
    B, S, D = x.shape
    x2 = x.reshape(B * S, D)
    z, xbc, dt, q, kvc, kv, gn, qm, gm = _in_proj(x2, norm_mix, w_in, nsa_q_norm, nsa_k_norm, mem_q_norm)
    ssd_n = _ssd(z.reshape(B, S, -1), xbc.reshape(B, S, -1), dt.reshape(B, S, -1), ssd_conv_w, ssd_conv_b,
                 ssd_dt_bias, ssd_a_log, ssd_d, ssd_norm)
    kvcmp = _cmp(kvc.reshape(B, S, -1), nsa_cmp_pe, nsa_cmp_w1, nsa_cmp_w2, nsa_k_norm[0])
    nsa_o = _nsa(q, kvcmp, kv, gn, B, S)
    k_mem, v_mem = _mem_kv(mem, mem_norm, w_mem_kv, mem_k_norm)
    h2 = _merge(x2, ssd_n.reshape(B * S, -1), nsa_o.reshape(B * S, -1), qm, gm, k_mem, v_mem,
                w_ssd_o, w_nsa_o, w_mem_o, w_out, S)
    out = _ffn(h2, norm_ffn, w_ffn_up, ffn_conv_w, ffn_conv_b, w_ffn_down, S)
    return out.reshape(B, S, D)


def kernel(x, mem, norm_mix, w_in, ssd_conv_w, ssd_conv_b, ssd_dt_bias, ssd_a_log, ssd_d, ssd_norm, w_ssd_o, nsa_q_norm, nsa_k_norm, nsa_cmp_pe, nsa_cmp_w1, nsa_cmp_w2, w_nsa_o, mem_norm, w_mem_kv, mem_q_norm, mem_k_norm, w_mem_o, w_out, norm_ffn, w_ffn_up, ffn_conv_w, ffn_conv_b, w_ffn_down):
    h = x
    for i in range(norm_mix.shape[0]):
        h = _layer(h, mem, norm_mix[i], w_in[i], ssd_conv_w[i], ssd_conv_b[i], ssd_dt_bias[i], ssd_a_log[i],
                   ssd_d[i], ssd_norm[i], w_ssd_o[i], nsa_q_norm[i], nsa_k_norm[i], nsa_cmp_pe[i],
                   nsa_cmp_w1[i], nsa_cmp_w2[i], w_nsa_o[i], mem_norm[i], w_mem_kv[i], mem_q_norm[i],
                   mem_k_norm[i], w_mem_o[i], w_out[i], norm_ffn[i], w_ffn_up[i], ffn_conv_w[i],
                   ffn_conv_b[i], w_ffn_down[i])
    return h
```

```python
import functools

import jax
import jax.numpy as jnp
from jax import lax
from jax.experimental import pallas as pl
from jax.experimental.pallas import tpu as pltpu

F32 = jnp.float32
BF16 = jnp.bfloat16

D_MODEL = 1024
SSD_HEADS = 16
SSD_HEAD_DIM = 64
SSD_GROUPS = 2
SSD_STATE = 128
SSD_CONV = 4
SSD_CHUNK = 128
SSD_INNER = SSD_HEADS * SSD_HEAD_DIM
SSD_XBC = SSD_INNER + 2 * SSD_GROUPS * SSD_STATE
NSA_HEADS = 8
NSA_KV_GROUPS = 2
NSA_HG = NSA_HEADS // NSA_KV_GROUPS
NSA_HEAD_DIM = 64
NSA_CMP_LEN = 32
NSA_CMP_STRIDE = 16
NSA_CMP_HIDDEN = 256
NSA_SLC_BLOCK = 64
NSA_TOP_N = 16
NSA_WINDOW = 512
NSA_INNER = NSA_HEADS * NSA_HEAD_DIM
NSA_KV = 3 * 2 * NSA_KV_GROUPS * NSA_HEAD_DIM
NSA_GATES = 3 * NSA_HEADS
MEM_HEADS = 4
MEM_HEAD_DIM = 128
MEM_INNER = MEM_HEADS * MEM_HEAD_DIM
N_BRANCH = 3
FFN_HIDDEN = 2816
FFN_CONV = 3
NORM_EPS = 1e-6
NEG = -1e30
BIG = 1e9

IN_SIZES = (SSD_INNER, SSD_XBC, SSD_HEADS, NSA_INNER, NSA_KV, NSA_GATES, MEM_INNER, N_BRANCH * D_MODEL)

LANES = 128
BF16_SUBLANES = 16
VMEM_LIMIT = 56 * 1024 * 1024

ROW_TILE = 256
NSA_TQ = 256
NSA_TK = 512
FFN_CHUNK = 256


def _dot(a, b):
    return jnp.dot(a, b, preferred_element_type=F32)


def _dot_nt(a, b):
    return lax.dot_general(a, b, (((1,), (1,)), ((), ())), preferred_element_type=F32)


def _split3(x):
    hi = x.astype(BF16)
    r1 = x - hi.astype(F32)
    mid = r1.astype(BF16)
    lo = (r1 - mid.astype(F32)).astype(BF16)
    return hi, mid, lo


def _dot3(x, w):
    hi, mid, lo = _split3(x)
    return _dot(hi, w) + _dot(mid, w) + _dot(lo, w)


def _silu(x):
    return x * jax.nn.sigmoid(x)


def _rms_rows(x, gain):
    return x * lax.rsqrt(jnp.mean(x * x, axis=-1, keepdims=True) + NORM_EPS) * gain


def _const_spec(shape):
    nd = len(shape)
    return pl.BlockSpec(shape, lambda *_: (0,) * nd, pipeline_mode=pl.Buffered(1))


def _params(sem):
    return pltpu.CompilerParams(dimension_semantics=sem, vmem_limit_bytes=VMEM_LIMIT)


def _head_rms(v, gain, d):
    if d == LANES:
        ss = jnp.sum(v * v, axis=-1, keepdims=True)
        return v * lax.rsqrt(ss / d + NORM_EPS) * gain
    lane = lax.broadcasted_iota(jnp.int32, v.shape, 1)
    head = lane < d
    ss = jnp.sum(jnp.where(head, v * v, 0.0), axis=-1, keepdims=True)
    return jnp.where(head, v * lax.rsqrt(ss / d + NORM_EPS) * gain, v)


def _in_proj_kernel(x_ref, g_ref, wz, wxbc, wdt, wq, wkvc, wkv, wgn, wqm, wgm, qn_ref, kn_ref, mqn_ref,
                    z_o, xbc_o, dt_o, q_o, kvc_o, kv_o, gn_o, qm_o, gm_o):
    xn = _rms_rows(x_ref[...], g_ref[...]).astype(BF16)
    z_o[...] = _dot(xn, wz[...]).astype(z_o.dtype)
    xbc_o[...] = _dot(xn, wxbc[...]).astype(xbc_o.dtype)
    dt_o[...] = _dot(xn, wdt[...])
    gn_o[...] = _dot(xn, wgn[...])
    gm_o[...] = _dot(xn, wgm[...]).astype(gm_o.dtype)
    kvc_o[...] = _dot(xn, wkvc[...]).astype(kvc_o.dtype)
    q = _dot(xn, wq[...])
    for h in range(NSA_HEADS):
        sl = slice(h * LANES, (h + 1) * LANES)
        q_o[:, sl] = _head_rms(q[:, sl], qn_ref[...], NSA_HEAD_DIM).astype(q_o.dtype)
    kv = _dot(xn, wkv[...])
    for j in range(2 * NSA_KV_GROUPS):
        sl = slice(j * LANES, (j + 1) * LANES)
        kv_o[:, sl] = _head_rms(kv[:, sl], kn_ref[j % 2:j % 2 + 1, :], NSA_HEAD_DIM).astype(kv_o.dtype)
    qm = _dot(xn, wqm[...])
    for h in range(MEM_HEADS):
        sl = slice(h * LANES, (h + 1) * LANES)
        qm_o[:, sl] = _head_rms(qm[:, sl], mqn_ref[...], MEM_HEAD_DIM).astype(qm_o.dtype)


def _in_proj(x2, norm_mix, w_in, nsa_q_norm, nsa_k_norm, mem_q_norm):
    T = x2.shape[0]
    TM = ROW_TILE
    o = [0]
    for s in IN_SIZES:
        o.append(o[-1] + s)
    wz = w_in[:, o[0]:o[1]]
    wxbc = w_in[:, o[1]:o[2]]
    wdt = jnp.pad(w_in[:, o[2]:o[3]], ((0, 0), (0, LANES - SSD_HEADS)))
    wq = w_in[:, o[3]:o[4]].reshape(D_MODEL, NSA_HEADS, NSA_HEAD_DIM)
    wq = jnp.pad(wq, ((0, 0), (0, 0), (0, LANES - NSA_HEAD_DIM))).reshape(D_MODEL, NSA_HEADS * LANES)
    wkv5 = w_in[:, o[4]:o[5]].reshape(D_MODEL, 3, 2, NSA_KV_GROUPS, NSA_HEAD_DIM)
    wkvc = wkv5[:, 0].transpose(0, 2, 1, 3).reshape(D_MODEL, NSA_KV_GROUPS * LANES)
    wkv = wkv5[:, 1:].transpose(0, 3, 1, 2, 4).reshape(D_MODEL, NSA_KV_GROUPS * 2 * LANES)
    wgn = w_in[:, o[5]:o[6]].reshape(D_MODEL, 3, NSA_KV_GROUPS, NSA_HG).transpose(0, 2, 1, 3)
    wgn = jnp.pad(wgn.reshape(D_MODEL, NSA_KV_GROUPS, 3 * NSA_HG), ((0, 0), (0, 0), (0, LANES - 3 * NSA_HG)))
    wgn = wgn.reshape(D_MODEL, NSA_KV_GROUPS * LANES)
    wqm = w_in[:, o[6]:o[7]]
    wgm = w_in[:, o[7]:o[8]]
    weights = [w.astype(BF16) for w in (wz, wxbc, wdt, wq, wkvc, wkv, wgn, wqm, wgm)]
    qn = jnp.pad(nsa_q_norm * (NSA_HEAD_DIM ** -0.5), (0, LANES - NSA_HEAD_DIM)).reshape(1, LANES)
    kn = jnp.concatenate([nsa_k_norm[1:3], jnp.ones((2, LANES - NSA_HEAD_DIM), F32)], axis=1)
    mqn = (mem_q_norm * (MEM_HEAD_DIM ** -0.5)).reshape(1, LANES)
    widths = [w.shape[1] for w in weights]
    dtypes = [BF16, BF16, F32, BF16, BF16, BF16, F32, BF16, BF16]
    out_shape = [jax.ShapeDtypeStruct((T, w), dt) for w, dt in zip(widths, dtypes)]
    row = lambda w: pl.BlockSpec((TM, w), lambda i: (i, 0))
    return pl.pallas_call(
        _in_proj_kernel,
        grid=(T // TM,),
        in_specs=[row(D_MODEL), _const_spec((1, D_MODEL))] + [_const_spec(w.shape) for w in weights]
        + [_const_spec((1, LANES)), _const_spec((2, LANES)), _const_spec((1, LANES))],
        out_specs=[row(w) for w in widths],
        out_shape=out_shape,
        compiler_params=_params(("parallel",)),
        name="in_proj",
    )(x2, norm_mix.reshape(1, D_MODEL), *weights, qn, kn, mqn)


def _softplus(x):
    return jnp.maximum(x, 0.0) + jnp.log1p(jnp.exp(-jnp.abs(x)))


def _ssd_kernel(z_ref, xbc_ref, halo_ref, dt_ref, cw_ref, cb_ref, dtb_ref, alog_ref, dexp_ref, ng_ref, r_ref,
                o_ref, state, ext):
    L = SSD_CHUNK
    HALO = BF16_SUBLANES
    c = pl.program_id(1)

    @pl.when(c == 0)
    def _():
        state[...] = jnp.zeros_like(state)

    halo = halo_ref[...].astype(F32) * jnp.where(c == 0, 0.0, 1.0)
    ext[0:HALO, :] = halo
    ext[HALO:HALO + L, :] = xbc_ref[...].astype(F32)
    acc = jnp.broadcast_to(cb_ref[...], (L, SSD_XBC))
    for k in range(SSD_CONV):
        acc = acc + cw_ref[k:k + 1, :] * ext[pl.ds(HALO - (SSD_CONV - 1) + k, L), :]
    xa = _silu(acc)
    xs = xa[:, :SSD_INNER]

    lane = lax.broadcasted_iota(jnp.int32, (L, LANES), 1)
    rowi = lax.broadcasted_iota(jnp.int32, (L, L), 0)
    coli = lax.broadcasted_iota(jnp.int32, (L, L), 1)
    tril = rowi >= coli
    tril_w = jnp.where(tril, 1.0, 0.0).astype(BF16)

    head_lane = lane < SSD_HEADS
    dt = jnp.where(head_lane, _softplus(dt_ref[...] + dtb_ref[...]), 0.0)
    d_a = dt * jnp.where(head_lane[0:1], -jnp.exp(alog_ref[...]), 0.0)
    cs = sum(_dot(tril_w, part) for part in _split3(d_a))
    cs_t = cs.T
    cs_last = cs[L - 1:L, :]
    ecs = jnp.exp(cs)
    decay = jnp.exp(cs_last - cs)
    r = r_ref[...]
    dt_x = _dot3(dt, r)
    ecs_x = _dot3(ecs, r)
    decay_x = _dot3(decay, r)

    xdt = xs * dt_x
    xdtd = (xdt * decay_x).astype(BF16)
    xdt16 = xdt.astype(BF16)
    y_skip = xs * dexp_ref[...]
    lane_lo = lane < SSD_HEAD_DIM

    hpg = SSD_HEADS // SSD_GROUPS
    gw = hpg * SSD_HEAD_DIM
    y_blocks = []
    for g in range(SSD_GROUPS):
        b_g = xa[:, SSD_INNER + g * SSD_STATE:SSD_INNER + (g + 1) * SSD_STATE]
        c_g = xa[:, SSD_INNER + (SSD_GROUPS + g) * SSD_STATE:SSD_INNER + (SSD_GROUPS + g + 1) * SSD_STATE]
        b16 = b_g.astype(BF16)
        c16 = c_g.astype(BF16)
        cb = _dot_nt(c16, b16)
        st = state[:, g * gw:(g + 1) * gw]
        y_off = _dot(c16, st.astype(BF16)) * ecs_x[:, g * gw:(g + 1) * gw]
        bt16 = b_g.T.astype(BF16)
        state[:, g * gw:(g + 1) * gw] = (st * ecs_x[L - 1:L, g * gw:(g + 1) * gw]
                                         + _dot(bt16, xdtd[:, g * gw:(g + 1) * gw]))
        for hp in range(hpg // 2):
            col = g * gw + hp * LANES
            xp = xdt16[:, col:col + LANES]
            pair = []
            for j in range(2):
                h = g * hpg + hp * 2 + j
                seg = cs[:, h:h + 1] - cs_t[h:h + 1, :]
                lm = jnp.where(tril, jnp.exp(jnp.where(tril, seg, 0.0)), 0.0)
                pair.append(_dot((cb * lm).astype(BF16), xp))
            y_diag = jnp.where(lane_lo, pair[0], pair[1])
            y_blocks.append(y_diag + y_off[:, hp * LANES:(hp + 1) * LANES] + y_skip[:, col:col + LANES])
    y = jnp.concatenate(y_blocks, axis=1)
    yz = y * _silu(z_ref[...].astype(F32))
    o_ref[...] = _rms_rows(yz, ng_ref[...]).astype(o_ref.dtype)


def _ssd(z, xbc, dt, conv_w, conv_b, dt_bias, a_log, d_skip, norm_g):
    B, S, _ = z.shape
    L = SSD_CHUNK
    HALO = BF16_SUBLANES
    pad = lambda v: jnp.pad(v, (0, LANES - SSD_HEADS)).reshape(1, LANES)
    d_exp = jnp.repeat(d_skip, SSD_HEAD_DIM).reshape(1, SSD_INNER)
    expand = (jnp.arange(LANES)[:, None] == (jnp.arange(SSD_INNER)[None, :] // SSD_HEAD_DIM)).astype(BF16)
    blk = lambda w: pl.BlockSpec((None, L, w), lambda b, c: (b, c, 0))
    halo_spec = pl.BlockSpec((None, HALO, SSD_XBC), lambda b, c: (b, jnp.maximum(c * (L // HALO) - 1, 0), 0))
    return pl.pallas_call(
        _ssd_kernel,
        grid=(B, S // L),
        in_specs=[blk(SSD_INNER), blk(SSD_XBC), halo_spec, blk(LANES),
                  _const_spec((SSD_CONV, SSD_XBC)), _const_spec((1, SSD_XBC)), _const_spec((1, LANES)),
                  _const_spec((1, LANES)), _const_spec((1, SSD_INNER)), _const_spec((1, SSD_INNER)),
                  _const_spec((LANES, SSD_INNER))],
        out_specs=blk(SSD_INNER),
        out_shape=jax.ShapeDtypeStruct((B, S, SSD_INNER), BF16),
        scratch_shapes=[pltpu.VMEM((SSD_STATE, SSD_INNER), F32), pltpu.VMEM((HALO + L, SSD_XBC), F32)],
        compiler_params=_params(("parallel", "arbitrary")),
        name="ssd",
    )(z, xbc, xbc, dt, conv_w, conv_b.reshape(1, SSD_XBC), pad(dt_bias), pad(a_log), d_exp,
      norm_g.reshape(1, SSD_INNER), expand)


def _cmp_kernel(kvc_ref, pea_ref, peb_ref, w1a_ref, w1b_ref, w2_ref, kn_ref, o_ref):
    n_chunk = kvc_ref.shape[0]
    per = NSA_CMP_STRIDE
    row = lax.broadcasted_iota(jnp.int32, (n_chunk, LANES), 0)
    for g in range(NSA_KV_GROUPS):
        x = jnp.concatenate([kvc_ref[:, (NSA_KV_GROUPS * l + g) * LANES:(NSA_KV_GROUPS * l + g + 1) * LANES]
                             for l in range(per)], axis=1).astype(F32)
        a = _dot((x + pea_ref[...]).astype(BF16), w1a_ref[...])
        b = _dot((x + peb_ref[...]).astype(BF16), w1b_ref[...])
        hid = _silu(a + pltpu.roll(b, n_chunk - 1, axis=0))
        cmp = _dot(hid.astype(BF16), w2_ref[...])
        cmp = _head_rms(cmp, kn_ref[...], NSA_HEAD_DIM)
        o_ref[g] = jnp.where(row < n_chunk - 1, cmp, 0.0).astype(o_ref.dtype)


def _cmp(kvc, nsa_cmp_pe, nsa_cmp_w1, nsa_cmp_w2, k_norm0):
    B, S, _ = kvc.shape
    n_chunk = S // NSA_CMP_STRIDE
    per = NSA_CMP_STRIDE
    dk = NSA_HEAD_DIM
    kvc_r = kvc.reshape(B, n_chunk, per * NSA_KV_GROUPS * LANES)
    pe = jnp.concatenate([nsa_cmp_pe[0], nsa_cmp_pe[1]], axis=1)
    pea = pe[:per].reshape(1, per * LANES)
    peb = pe[per:].reshape(1, per * LANES)
    w1 = nsa_cmp_w1.reshape(2, NSA_CMP_LEN, dk, NSA_CMP_HIDDEN)
    zero = jnp.zeros((NSA_CMP_LEN, dk, NSA_CMP_HIDDEN), F32)
    w1bd = jnp.concatenate([jnp.concatenate([w1[0], zero], axis=2), jnp.concatenate([zero, w1[1]], axis=2)], axis=1)
    w1a = w1bd[:per].reshape(per * LANES, 2 * NSA_CMP_HIDDEN).astype(BF16)
    w1b = w1bd[per:].reshape(per * LANES, 2 * NSA_CMP_HIDDEN).astype(BF16)
    z2 = jnp.zeros((NSA_CMP_HIDDEN, dk), F32)
    w2bd = jnp.concatenate([jnp.concatenate([nsa_cmp_w2[0], z2], axis=1),
                            jnp.concatenate([z2, nsa_cmp_w2[1]], axis=1)], axis=0).astype(BF16)
    kn = jnp.concatenate([k_norm0, jnp.ones((LANES - dk,), F32)]).reshape(1, LANES)
    return pl.pallas_call(
        _cmp_kernel,
        grid=(B,),
        in_specs=[pl.BlockSpec((None, n_chunk, per * NSA_KV_GROUPS * LANES), lambda b: (b, 0, 0)),
                  _const_spec(pea.shape), _const_spec(peb.shape), _const_spec(w1a.shape), _const_spec(w1b.shape),
                  _const_spec(w2bd.shape), _const_spec(kn.shape)],
        out_specs=pl.BlockSpec((None, NSA_KV_GROUPS, n_chunk, LANES), lambda b: (b, 0, 0, 0)),
        out_shape=jax.ShapeDtypeStruct((B, NSA_KV_GROUPS, n_chunk, LANES), BF16),
        compiler_params=_params(("parallel",)),
        name="nsa_cmp",
    )(kvc_r, pea, peb, w1a, w1b, w2bd, kn)


def _top_blocks(imp, jt):
    n_slc, TQ = imp.shape
    blk_q = lax.broadcasted_iota(jnp.int32, (n_slc, TQ), 0)
    imp = jnp.where(blk_q == 0, BIG, imp)
    imp = jnp.where(blk_q == jt, BIG, imp)
    imp = jnp.where(blk_q == jt - 1, BIG, imp)
    imp = jnp.where(blk_q > jt, NEG, imp)
    SUB = 8
    row8 = lax.broadcasted_iota(jnp.int32, (SUB, 1), 0)
    groups = [imp[k * SUB:(k + 1) * SUB] for k in range(n_slc // SUB)]
    ranks = [jnp.zeros((SUB, TQ), F32) for _ in groups]
    for i in range(n_slc):
        vi = jnp.broadcast_to(imp[i:i + 1, :], (SUB, TQ))
        for k, grp in enumerate(groups):
            if k * SUB > i:
                ahead = jnp.where(vi >= grp, 1.0, 0.0)
            elif k * SUB + SUB - 1 < i:
                ahead = jnp.where(vi > grp, 1.0, 0.0)
            else:
                ahead = jnp.where(row8 + k * SUB > i, jnp.where(vi >= grp, 1.0, 0.0), jnp.where(vi > grp, 1.0, 0.0))
            ranks[k] = ranks[k] + ahead
    rank = jnp.concatenate(ranks, axis=0)
    return jnp.where(rank < NSA_TOP_N, jnp.where(blk_q <= jt, 1.0, 0.0), 0.0)


def _nsa_kernel(q_ref, kvc_ref, kv_ref, gate_ref, ov_ref, o_ref, k2c, vct, k2s, vst, k2w, vwt, *, seq):
    TQ, TK, W = NSA_TQ, NSA_TK, NSA_WINDOW
    G, HG, DK = NSA_KV_GROUPS, NSA_HG, NSA_HEAD_DIM
    PAIR = 2
    n_cmp_pad = kvc_ref.shape[1]
    n_slc = seq // NSA_SLC_BLOCK
    t0 = pl.program_id(1) * TQ
    slopes = [[2.0 ** -(g * HG + h + 1) for h in range(HG)] for g in range(G)]

    @pl.when(pl.program_id(1) == 0)
    def _build_keys():
        lane = lax.broadcasted_iota(jnp.int32, (LANES, LANES), 1)
        row = lax.broadcasted_iota(jnp.int32, (LANES, LANES), 0)

        def with_pos(kv, pos):
            feat = jnp.where(lane == DK, (pos // NSA_SLC_BLOCK).astype(F32),
                             jnp.where(lane == DK + 1, (pos % NSA_SLC_BLOCK).astype(F32), 0.0))
            return jnp.where(lane < DK, kv, feat).astype(BF16)

        for g in range(G):
            for c in range(n_cmp_pad // LANES):
                rows = slice(c * LANES, (c + 1) * LANES)
                kv = kvc_ref[g, rows, :].astype(F32)
                k2c[g, rows, :] = with_pos(kv, (row + c * LANES) * NSA_CMP_STRIDE + (NSA_CMP_LEN - 1))
                vct[g, :, rows] = kv.T[DK:].astype(BF16)

        def chunk(c, carry):
            r0 = pl.multiple_of(c * LANES, LANES)
            pos = row + r0
            one_hot = jnp.where(lane == pos // NSA_SLC_BLOCK, 1.0, 0.0).astype(BF16)
            for g in range(G):
                kv = kv_ref[pl.ds(r0, LANES), 2 * g * LANES:(2 * g + 1) * LANES].astype(F32)
                k2s[g, pl.ds(r0, LANES), 0:LANES] = with_pos(kv, pos)
                k2s[g, pl.ds(r0, LANES), LANES:2 * LANES] = one_hot
                vst[g, :, pl.ds(r0, LANES)] = kv.T[DK:].astype(BF16)
                kvw = kv_ref[pl.ds(r0, LANES), (2 * g + 1) * LANES:(2 * g + 2) * LANES].astype(F32)
                k2w[g, pl.ds(r0, LANES), :] = with_pos(kvw, pos)
                vwt[g, :, pl.ds(r0, LANES)] = kvw.T[DK:].astype(BF16)
            return carry

        lax.fori_loop(0, seq // LANES, chunk, 0)

    lane_q = lax.broadcasted_iota(jnp.int32, (TQ, LANES), 1)
    q1 = [[None] * HG for _ in range(G)]
    for g in range(G):
        for h in range(HG):
            qf = q_ref[:, (g * HG + h) * LANES:(g * HG + h + 1) * LANES].astype(F32)
            qf = jnp.where(lane_q == DK, NSA_SLC_BLOCK * slopes[g][h], jnp.where(lane_q == DK + 1, slopes[g][h], qf))
            q1[g][h] = qf.astype(BF16)
    chains = [(g, hp) for g in range(G) for hp in range(HG // PAIR)]
    q1c = {(g, hp): jnp.concatenate(q1[g][hp * PAIR:(hp + 1) * PAIR], axis=0) for g, hp in chains}
    tq = t0 + lax.broadcasted_iota(jnp.int32, (1, TQ), 1)
    jt = tq // NSA_SLC_BLOCK

    def tile_pair(mask):
        return jnp.concatenate([mask] * PAIR, axis=1)

    cend = lax.broadcasted_iota(jnp.int32, (n_cmp_pad, 1), 0) * NSA_CMP_STRIDE + (NSA_CMP_LEN - 1)
    valid = (tq - cend) >= 0
    o_cmp = {}
    psum = [jnp.zeros((n_cmp_pad, TQ), F32) for _ in range(G)]
    sc_all = [_dot_nt(k2c[g], q1c[g, hp]) for g, hp in chains]
    for (g, hp), sct in zip(chains, sc_all):
        p_list = []
        for j in range(PAIR):
            s = jnp.where(valid, sct[:, j * TQ:(j + 1) * TQ], NEG)
            m = jnp.max(s, axis=0, keepdims=True)
            p = jnp.where(valid, jnp.exp(s - m), 0.0)
            l = jnp.sum(p, axis=0, keepdims=True)
            p = p / jnp.where(l > 0.0, l, 1.0)
            psum[g] = psum[g] + p
            p_list.append(p.astype(BF16))
        o_cmp[g, hp] = _dot(vct[g], jnp.concatenate(p_list, axis=1))
    q2c = {}
    for g in range(G):
        imp = sum(_dot(ov_ref[...], part) for part in _split3(psum[g]))
        sel = _top_blocks(imp[0:n_slc], jt)
        if n_slc < LANES:
            sel = jnp.concatenate([sel, jnp.zeros((LANES - n_slc, TQ), F32)], axis=0)
        sel_neg = jnp.where(sel.T > 0.5, 0.0, -(2.0 ** 100)).astype(BF16)
        for hp in range(HG // PAIR):
            q2c[g, hp] = jnp.concatenate([jnp.concatenate([q1[g][hp * PAIR + j], sel_neg], axis=1)
                                          for j in range(PAIR)], axis=0)

    def slc_step(k0, carry, causal_add):
        out = []
        scores = [_dot_nt(k2s[g, pl.ds(k0, TK), :], q2c[g, hp]) for g, hp in chains]
        for (g, hp), (m, l, acc), s in zip(chains, carry, scores):
            if causal_add is not None:
                s = s + causal_add
            m_new = jnp.maximum(m, jnp.max(s, axis=0, keepdims=True))
            alpha = jnp.exp(m - m_new)
            p = jnp.exp(s - m_new)
            l = alpha * l + jnp.sum(p, axis=0, keepdims=True)
            acc = alpha * acc + _dot(vst[g, :, pl.ds(k0, TK)], p.astype(BF16))
            out.append((m_new, l, acc))
        return tuple(out)

    n_past = t0 // TK
    init = tuple((jnp.full((1, PAIR * TQ), NEG, F32), jnp.zeros((1, PAIR * TQ), F32),
                  jnp.zeros((DK, PAIR * TQ), F32)) for _ in chains)
    carry = lax.fori_loop(0, n_past, lambda kt, c: slc_step(pl.multiple_of(kt * TK, TK), c, None), init)
    kd = pl.multiple_of(n_past * TK, TK)
    kpos = kd + lax.broadcasted_iota(jnp.int32, (TK, 1), 0)
    final = slc_step(kd, carry, tile_pair(jnp.where(kpos <= tq, 0.0, NEG)))
    o_slc = {ch: acc / l for ch, (_, l, acc) in zip(chains, final)}

    band = W + TQ
    w0 = pl.multiple_of(jnp.maximum(t0 - W, 0), TQ)
    dw = tq - (w0 + lax.broadcasted_iota(jnp.int32, (band, 1), 0))
    in_win = tile_pair(jnp.where(dw >= 0, jnp.where(dw < W, 0.0, NEG), NEG))
    o_win = {}
    sw_all = [_dot_nt(k2w[g, pl.ds(w0, band), :], q1c[g, hp]) for g, hp in chains]
    for (g, hp), sw in zip(chains, sw_all):
        sw = sw + in_win
        pw = jnp.exp(sw - jnp.max(sw, axis=0, keepdims=True))
        o_win[g, hp] = _dot(vwt[g, :, pl.ds(w0, band)], pw.astype(BF16)) / jnp.sum(pw, axis=0, keepdims=True)

    comb = []
    for g in range(G):
        gt = jax.nn.sigmoid(gate_ref[:, g * LANES:(g + 1) * LANES]).T
        for h in range(HG):
            ch = (g, h // PAIR)
            cols = slice((h % PAIR) * TQ, (h % PAIR + 1) * TQ)
            comb.append(gt[h:h + 1] * o_cmp[ch][:, cols] + gt[HG + h:HG + h + 1] * o_slc[ch][:, cols]
                        + gt[2 * HG + h:2 * HG + h + 1] * o_win[ch][:, cols])
    o_ref[...] = jnp.concatenate(comb, axis=0).T.astype(o_ref.dtype)


def _nsa(q, kvcmp, kv, gates, B, S):
    TQ = NSA_TQ
    G = NSA_KV_GROUPS
    n_cmp_pad = S // NSA_CMP_STRIDE
    n_slc = S // NSA_SLC_BLOCK
    cs = jnp.arange(n_cmp_pad) * NSA_CMP_STRIDE
    ce = cs + NSA_CMP_LEN - 1
    ss = jnp.arange(LANES) * NSA_SLC_BLOCK
    overlap = jnp.clip(jnp.minimum(ce[None, :], ss[:, None] + NSA_SLC_BLOCK - 1)
                       - jnp.maximum(cs[None, :], ss[:, None]) + 1, 0).astype(F32) / NSA_CMP_LEN
    valid_pair = (jnp.arange(LANES)[:, None] < n_slc) & (jnp.arange(n_cmp_pad)[None, :] < n_cmp_pad - 1)
    ov_t = jnp.where(valid_pair, overlap, 0.0).astype(BF16)
    q3 = q.reshape(B, S, NSA_HEADS * LANES)
    kv3 = kv.reshape(B, S, G * 2 * LANES)
    g3 = gates.reshape(B, S, G * LANES)
    return pl.pallas_call(
        functools.partial(_nsa_kernel, seq=S),
        grid=(B, S // TQ),
        in_specs=[pl.BlockSpec((None, TQ, NSA_HEADS * LANES), lambda b, i: (b, i, 0)),
                  pl.BlockSpec((None, G, n_cmp_pad, LANES), lambda b, i: (b, 0, 0, 0)),
                  pl.BlockSpec((None, S, G * 2 * LANES), lambda b, i: (b, 0, 0)),
                  pl.BlockSpec((None, TQ, G * LANES), lambda b, i: (b, i, 0)),
                  _const_spec(ov_t.shape)],
        out_specs=pl.BlockSpec((None, TQ, NSA_INNER), lambda b, i: (b, i, 0)),
        out_shape=jax.ShapeDtypeStruct((B, S, NSA_INNER), BF16),
        scratch_shapes=[pltpu.VMEM((G, n_cmp_pad, LANES), BF16), pltpu.VMEM((G, NSA_HEAD_DIM, n_cmp_pad), BF16),
                        pltpu.VMEM((G, S, 2 * LANES), BF16), pltpu.VMEM((G, NSA_HEAD_DIM, S), BF16),
                        pltpu.VMEM((G, S, LANES), BF16), pltpu.VMEM((G, NSA_HEAD_DIM, S), BF16)],
        compiler_params=_params(("parallel", "arbitrary")),
        name="nsa_attn",
    )(q3, kvcmp, kv3, g3, ov_t)


def _mem_kv_kernel(mem_ref, g_ref, w_ref, kn_ref, k_o, v_o):
    mn = _rms_rows(mem_ref[...], g_ref[...]).astype(BF16)
    kv = _dot(mn, w_ref[...])
    for h in range(MEM_HEADS):
        sl = slice(h * LANES, (h + 1) * LANES)
        k_o[:, sl] = _head_rms(kv[:, sl], kn_ref[...], MEM_HEAD_DIM).astype(k_o.dtype)
    v_o[...] = kv[:, MEM_INNER:].astype(v_o.dtype)


def _mem_kv(mem, mem_norm, w_mem_kv, mem_k_norm):
    B, M, _ = mem.shape
    spec_o = pl.BlockSpec((None, M, MEM_INNER), lambda b: (b, 0, 0))
    return pl.pallas_call(
        _mem_kv_kernel,
        grid=(B,),
        in_specs=[pl.BlockSpec((None, M, D_MODEL), lambda b: (b, 0, 0)), _const_spec((1, D_MODEL)),
                  _const_spec((D_MODEL, 2 * MEM_INNER)), _const_spec((1, LANES))],
        out_specs=[spec_o, spec_o],
        out_shape=[jax.ShapeDtypeStruct((B, M, MEM_INNER), BF16)] * 2,
        compiler_params=_params(("parallel",)),
        name="mem_kv",
    )(mem, mem_norm.reshape(1, D_MODEL), w_mem_kv.astype(BF16), mem_k_norm.reshape(1, LANES))


def _merge_kernel(x_ref, ssd_ref, nsa_ref, qm_ref, gm_ref, km_ref, vm_ref, wso, wno, wmo, wout, h_o):
    o_mem = []
    for h in range(MEM_HEADS):
        sl = slice(h * LANES, (h + 1) * LANES)
        s = _dot_nt(qm_ref[:, sl], km_ref[:, sl])
        p = jnp.exp(s - jnp.max(s, axis=-1, keepdims=True))
        o = _dot(p.astype(BF16), vm_ref[:, sl]) / jnp.sum(p, axis=-1, keepdims=True)
        o_mem.append(o.astype(BF16))
    y_mem = _dot(jnp.concatenate(o_mem, axis=1), wmo[...])
    y_ssd = _dot(ssd_ref[...], wso[...])
    y_nsa = _dot(nsa_ref[...], wno[...])
    gate = lambda j: jax.nn.sigmoid(gm_ref[:, j * D_MODEL:(j + 1) * D_MODEL].astype(F32))
    mixed = gate(0) * y_ssd + gate(1) * y_nsa + gate(2) * y_mem
    h_o[...] = x_ref[...] + _dot(mixed.astype(BF16), wout[...])


def _merge(x2, ssd_n, nsa_o, q_mem, g_merge, k_mem, v_mem, w_ssd_o, w_nsa_o, w_mem_o, w_out, S):
    T = x2.shape[0]
    TM = ROW_TILE
    M = k_mem.shape[1]
    per_batch = S // TM
    row = lambda w: pl.BlockSpec((TM, w), lambda i: (i, 0))
    mem_spec = pl.BlockSpec((None, M, MEM_INNER), lambda i: (i // per_batch, 0, 0))
    weights = [w.astype(BF16) for w in (w_ssd_o, w_nsa_o, w_mem_o, w_out)]
    return pl.pallas_call(
        _merge_kernel,
        grid=(T // TM,),
        in_specs=[row(D_MODEL), row(SSD_INNER), row(NSA_INNER), row(MEM_INNER), row(N_BRANCH * D_MODEL),
                  mem_spec, mem_spec] + [_const_spec(w.shape) for w in weights],
        out_specs=row(D_MODEL),
        out_shape=jax.ShapeDtypeStruct((T, D_MODEL), F32),
        compiler_params=_params(("parallel",)),
        name="merge",
    )(x2, ssd_n, nsa_o, q_mem, g_merge, k_mem, v_mem, *weights)


def _ffn_kernel(h_ref, halo_ref, g_ref, wup, cw_ref, cb_ref, wdown, o_ref, *, per_batch):
    TM = h_ref.shape[0]
    HALO = halo_ref.shape[0]
    CW = FFN_CHUNK
    first = pl.program_id(0) % per_batch == 0
    h = h_ref[...]
    halo_n = _rms_rows(halo_ref[...], g_ref[...]) * jnp.where(first, 0.0, 1.0)
    hn = jnp.concatenate([halo_n, _rms_rows(h, g_ref[...])], axis=0).astype(BF16)
    n_chunks = FFN_HIDDEN // CW

    def up_proj(c):
        return [_dot(hn, wup[:, half * FFN_HIDDEN + c * CW:half * FFN_HIDDEN + (c + 1) * CW]) for half in range(2)]

    acc = jnp.zeros((TM, D_MODEL), F32)
    ext = up_proj(0)
    for c in range(n_chunks):
        ext_next = up_proj(c + 1) if c + 1 < n_chunks else None
        conv = []
        for half in range(2):
            col = half * FFN_HIDDEN + c * CW
            u = jnp.broadcast_to(cb_ref[:, col:col + CW], (TM, CW))
            for k in range(FFN_CONV):
                start = HALO - (FFN_CONV - 1) + k
                u = u + cw_ref[k:k + 1, col:col + CW] * ext[half][start:start + TM]
            conv.append(u)
        act = (_silu(conv[0]) * conv[1]).astype(BF16)
        acc = acc + _dot(act, wdown[c * CW:(c + 1) * CW, :])
        ext = ext_next
    o_ref[...] = h + acc


def _ffn(h2, norm_ffn, w_ffn_up, ffn_conv_w, ffn_conv_b, w_ffn_down, S):
    T = h2.shape[0]
    TM = ROW_TILE
    HALO = BF16_SUBLANES
    per_batch = S // TM
    return pl.pallas_call(
        functools.partial(_ffn_kernel, per_batch=per_batch),
        grid=(T // TM,),
        in_specs=[pl.BlockSpec((TM, D_MODEL), lambda i: (i, 0)),
                  pl.BlockSpec((HALO, D_MODEL), lambda i: (jnp.maximum(i * (TM // HALO) - 1, 0), 0)),
                  _const_spec((1, D_MODEL)), _const_spec((D_MODEL, 2 * FFN_HIDDEN)),
                  _const_spec((FFN_CONV, 2 * FFN_HIDDEN)), _const_spec((1, 2 * FFN_HIDDEN)),
                  _const_spec((FFN_HIDDEN, D_MODEL))],
        out_specs=pl.BlockSpec((TM, D_MODEL), lambda i: (i, 0)),
        out_shape=jax.ShapeDtypeStruct((T, D_MODEL), F32),
        compiler_params=_params(("parallel",)),
        name="ffn",
    )(h2, h2, norm_ffn.reshape(1, D_MODEL), w_ffn_up.astype(BF16), ffn_conv_w,
      ffn_conv_b.reshape(1, 2 * FFN_HIDDEN), w_ffn_down.astype(BF16))


def _layer(x, mem, norm_mix, w_in, ssd_conv_w, ssd_conv_b, ssd_dt_bias, ssd_a_log, ssd_d, ssd_norm, w_ssd_o,
           nsa_q_norm, nsa_k_norm, nsa_cmp_pe, nsa_cmp_w1, nsa_cmp_w2, w_nsa_o, mem_norm, w_mem_kv,
           mem_q_norm, mem_k_norm, w_mem_o, w_out, norm_ffn, w_ffn_up, ffn_conv_w, ffn_conv_b, w_ffn_down):
    B, S, D = x.shape
    x2 = x.reshape(B * S, D)
    z, xbc, dt, q, kvc, kv, gn, qm, gm = _in_proj(x2, norm_mix, w_in, nsa_q_norm, nsa_k_norm, mem_q_norm)
    ssd_n = _ssd(z.reshape(B, S, -1), xbc.reshape(B, S, -1), dt.reshape(B, S, -1), ssd_conv_w, ssd_conv_b,
                 ssd_dt_bias, ssd_a_log, ssd_d, ssd_norm)
    kvcmp = _cmp(kvc.reshape(B, S, -1), nsa_cmp_pe, nsa_cmp_w1, nsa_cmp_w2, nsa_k_norm[0])
    nsa_o = _nsa(q, kvcmp, kv, gn, B, S)
    k_mem, v_mem = _mem_kv(mem, mem_norm, w_mem_kv, mem_k_norm)
    h2 = _merge(x2, ssd_n.reshape(B * S, -1), nsa_o.reshape(B * S, -1), qm, gm, k_mem, v_mem,
                w_ssd_o, w_nsa_o, w_mem_o, w_out, S)
    out = _ffn(h2, norm_ffn, w_ffn_up, ffn_conv_w, ffn_conv_b, w_ffn_down, S)
    return out.reshape(B, S, D)


def kernel(x, mem, norm_mix, w_in, ssd_conv_w, ssd_conv_b, ssd_dt_bias, ssd_a_log, ssd_d, ssd_norm, w_ssd_o, nsa_q_norm, nsa_k_norm, nsa_cmp_pe, nsa_cmp_w1, nsa_cmp_w2, w_nsa_o, mem_norm, w_mem_kv, mem_q_norm, mem_k_norm, w_mem_o, w_out, norm_ffn, w_ffn_up, ffn_conv_w, ffn_conv_b, w_ffn_down):
    h = x
    for i in range(norm_mix.shape[0]):
        h = _layer(h, mem, norm_mix[i], w_in[i], ssd_conv_w[i], ssd_conv_b[i], ssd_dt_bias[i], ssd_a_log[i],
                   ssd_d[i], ssd_norm[i], w_ssd_o[i], nsa_q_norm[i], nsa_k_norm[i], nsa_cmp_pe[i],
                   nsa_cmp_w1[i], nsa_cmp_w2[i], w_nsa_o[i], mem_norm[i], w_mem_kv[i], mem_q_norm[i],
                   mem_k_norm[i], w_mem_o[i], w_out[i], norm_ffn[i], w_ffn_up[i], ffn_conv_w[i],
                   ffn_conv_b[i], w_ffn_down[i])
    return h
```

```python
import functools

import jax
import jax.numpy as jnp
import numpy as np
from jax import lax
from jax.experimental import pallas as pl
from jax.experimental.pallas import tpu as pltpu

F32 = jnp.float32
BF16 = jnp.bfloat16

D_MODEL = 1024
SSD_HEADS = 16
SSD_HEAD_DIM = 64
SSD_GROUPS = 2
SSD_STATE = 128
SSD_CONV = 4
SSD_CHUNK = 128
SSD_INNER = SSD_HEADS * SSD_HEAD_DIM
SSD_XBC = SSD_INNER + 2 * SSD_GROUPS * SSD_STATE
NSA_HEADS = 8
NSA_KV_GROUPS = 2
NSA_HG = NSA_HEADS // NSA_KV_GROUPS
NSA_HEAD_DIM = 64
NSA_CMP_LEN = 32
NSA_CMP_STRIDE = 16
NSA_CMP_HIDDEN = 256
NSA_SLC_BLOCK = 64
NSA_TOP_N = 16
NSA_WINDOW = 512
NSA_INNER = NSA_HEADS * NSA_HEAD_DIM
NSA_KV = 3 * 2 * NSA_KV_GROUPS * NSA_HEAD_DIM
NSA_GATES = 3 * NSA_HEADS
MEM_HEADS = 4
MEM_HEAD_DIM = 128
MEM_INNER = MEM_HEADS * MEM_HEAD_DIM
N_BRANCH = 3
FFN_HIDDEN = 2816
FFN_CONV = 3
NORM_EPS = 1e-6
NEG = -1e30
BIG = 1e9

IN_SIZES = (SSD_INNER, SSD_XBC, SSD_HEADS, NSA_INNER, NSA_KV, NSA_GATES, MEM_INNER, N_BRANCH * D_MODEL)

LANES = 128
BF16_SUBLANES = 16
VMEM_LIMIT = 56 * 1024 * 1024


def _bf16_terms(x, n):
    terms, rest = [], np.float32(x)
    for _ in range(n):
        t = np.float32(np.asarray(rest).astype(jnp.bfloat16))
        terms.append(float(t))
        rest = np.float32(rest - t)
    return tuple(terms)


LOG2E = float(np.float32(np.log2(np.e)))
LOG2E_TERMS = _bf16_terms(LOG2E, 3)

ROW_TILE = 512
NSA_TQ = 256
NSA_TK = 512
FFN_CHUNK = 256
FFN_DOWN_GROUP = 6


def _dot(a, b):
    return jnp.dot(a, b, preferred_element_type=F32)


def _dot_nt(a, b):
    return lax.dot_general(a, b, (((1,), (1,)), ((), ())), preferred_element_type=F32)


def _split3(x):
    hi = x.astype(BF16)
    r1 = x - hi.astype(F32)
    mid = r1.astype(BF16)
    lo = (r1 - mid.astype(F32)).astype(BF16)
    return hi, mid, lo


def _dot3(x, w):
    hi, mid, lo = _split3(x)
    return _dot(hi, w) + _dot(mid, w) + _dot(lo, w)


def _silu(x):
    return x * jax.nn.sigmoid(x)


def _rms_rows(x, gain):
    return x * lax.rsqrt(jnp.mean(x * x, axis=-1, keepdims=True) + NORM_EPS) * gain


def _const_spec(shape):
    nd = len(shape)
    return pl.BlockSpec(shape, lambda *_: (0,) * nd, pipeline_mode=pl.Buffered(1))


def _params(sem, flags=None):
    return pltpu.CompilerParams(dimension_semantics=sem, vmem_limit_bytes=VMEM_LIMIT, flags=flags)


def _head_rms(v, gain, d):
    if d == LANES:
        ss = jnp.sum(v * v, axis=-1, keepdims=True)
        return v * lax.rsqrt(ss / d + NORM_EPS) * gain
    lane = lax.broadcasted_iota(jnp.int32, v.shape, 1)
    head = lane < d
    ss = jnp.sum(jnp.where(head, v * v, 0.0), axis=-1, keepdims=True)
    return jnp.where(head, v * lax.rsqrt(ss / d + NORM_EPS) * gain, v)


def _in_proj_kernel(x_ref, g_ref, wz, wxbc, wdt, wq, wkvc, wkv, wgn, wqm, wgm, qn_ref, kn_ref, mqn_ref,
                    z_o, xbc_o, dt_o, q_o, kvc_o, kv_o, gn_o, qm_o, gm_o):
    xn = _rms_rows(x_ref[...], g_ref[...]).astype(BF16)
    z_o[...] = _dot(xn, wz[...]).astype(z_o.dtype)
    xbc_o[...] = _dot(xn, wxbc[...]).astype(xbc_o.dtype)
    dt_o[...] = _dot(xn, wdt[...])
    gn_o[...] = _dot(xn, wgn[...])
    gm_o[...] = _dot(xn, wgm[...]).astype(gm_o.dtype)
    kvc_o[...] = _dot(xn, wkvc[...]).astype(kvc_o.dtype)
    q = _dot(xn, wq[...])
    for h in range(NSA_HEADS):
        sl = slice(h * LANES, (h + 1) * LANES)
        q_o[:, sl] = _head_rms(q[:, sl], qn_ref[...], NSA_HEAD_DIM).astype(q_o.dtype)
    kv = _dot(xn, wkv[...])
    for j in range(2 * NSA_KV_GROUPS):
        sl = slice(j * LANES, (j + 1) * LANES)
        kv_o[:, sl] = _head_rms(kv[:, sl], kn_ref[j % 2:j % 2 + 1, :], NSA_HEAD_DIM).astype(kv_o.dtype)
    qm = _dot(xn, wqm[...])
    for h in range(MEM_HEADS):
        sl = slice(h * LANES, (h + 1) * LANES)
        qm_o[:, sl] = _head_rms(qm[:, sl], mqn_ref[...], MEM_HEAD_DIM).astype(qm_o.dtype)


def _in_proj(x2, norm_mix, w_in, nsa_q_norm, nsa_k_norm, mem_q_norm):
    T = x2.shape[0]
    TM = ROW_TILE
    o = [0]
    for s in IN_SIZES:
        o.append(o[-1] + s)
    wz = w_in[:, o[0]:o[1]]
    wxbc = w_in[:, o[1]:o[2]]
    wdt = jnp.pad(w_in[:, o[2]:o[3]], ((0, 0), (0, LANES - SSD_HEADS)))
    wq = w_in[:, o[3]:o[4]].reshape(D_MODEL, NSA_HEADS, NSA_HEAD_DIM)
    wq = jnp.pad(wq, ((0, 0), (0, 0), (0, LANES - NSA_HEAD_DIM))).reshape(D_MODEL, NSA_HEADS * LANES)
    wkv5 = w_in[:, o[4]:o[5]].reshape(D_MODEL, 3, 2, NSA_KV_GROUPS, NSA_HEAD_DIM)
    wkvc = wkv5[:, 0].transpose(0, 2, 1, 3).reshape(D_MODEL, NSA_KV_GROUPS * LANES)
    wkv = wkv5[:, 1:].transpose(0, 3, 1, 2, 4).reshape(D_MODEL, NSA_KV_GROUPS * 2 * LANES)
    wgn = w_in[:, o[5]:o[6]].reshape(D_MODEL, 3, NSA_KV_GROUPS, NSA_HG).transpose(0, 2, 1, 3)
    wgn = jnp.pad(wgn.reshape(D_MODEL, NSA_KV_GROUPS, 3 * NSA_HG), ((0, 0), (0, 0), (0, LANES - 3 * NSA_HG)))
    wgn = wgn.reshape(D_MODEL, NSA_KV_GROUPS * LANES)
    wqm = w_in[:, o[6]:o[7]]
    wgm = w_in[:, o[7]:o[8]]
    weights = [w.astype(BF16) for w in (wz, wxbc, wdt, wq, wkvc, wkv, wgn, wqm, wgm)]
    qn = jnp.pad(nsa_q_norm * (NSA_HEAD_DIM ** -0.5 * LOG2E), (0, LANES - NSA_HEAD_DIM)).reshape(1, LANES)
    kn = jnp.concatenate([nsa_k_norm[1:3], jnp.ones((2, LANES - NSA_HEAD_DIM), F32)], axis=1)
    mqn = (mem_q_norm * (MEM_HEAD_DIM ** -0.5 * LOG2E)).reshape(1, LANES)
    widths = [w.shape[1] for w in weights]
    dtypes = [BF16, BF16, F32, BF16, BF16, BF16, F32, BF16, BF16]
    out_shape = [jax.ShapeDtypeStruct((T, w), dt) for w, dt in zip(widths, dtypes)]
    row = lambda w: pl.BlockSpec((TM, w), lambda i: (i, 0))
    return pl.pallas_call(
        _in_proj_kernel,
        grid=(T // TM,),
        in_specs=[row(D_MODEL), _const_spec((1, D_MODEL))] + [_const_spec(w.shape) for w in weights]
        + [_const_spec((1, LANES)), _const_spec((2, LANES)), _const_spec((1, LANES))],
        out_specs=[row(w) for w in widths],
        out_shape=out_shape,
        compiler_params=_params(("parallel",)),
        name="in_proj",
    )(x2, norm_mix.reshape(1, D_MODEL), *weights, qn, kn, mqn)


def _softplus(x):
    return jnp.maximum(x, 0.0) + jnp.log1p(jnp.exp(-jnp.abs(x)))


def _ssd_kernel(z_ref, xbc_ref, halo_ref, dt_ref, cw_ref, cb_ref, dtb_ref, alog_ref, dexp_ref, ng_ref, r_ref,
                o_ref, state):
    L = SSD_CHUNK
    HALO = BF16_SUBLANES
    c = pl.program_id(1)

    @pl.when(c == 0)
    def _():
        state[...] = jnp.zeros_like(state)

    halo = jnp.where(c == 0, jnp.zeros_like(halo_ref[...]), halo_ref[...])
    xe = jnp.concatenate([halo, xbc_ref[...]], axis=0)
    src = lax.broadcasted_iota(jnp.int32, (L, HALO + L), 1) - lax.broadcasted_iota(jnp.int32, (L, HALO + L), 0)
    acc = cb_ref[...] + cw_ref[SSD_CONV - 1:SSD_CONV, :] * xbc_ref[...].astype(F32)
    for k in range(SSD_CONV - 1):
        shift = jnp.where(src == HALO - (SSD_CONV - 1) + k, 1.0, 0.0).astype(BF16)
        acc = acc + cw_ref[k:k + 1, :] * _dot(shift, xe)
    xa = _silu(acc)
    xs = xa[:, :SSD_INNER]

    lane = lax.broadcasted_iota(jnp.int32, (L, LANES), 1)
    rowi = lax.broadcasted_iota(jnp.int32, (L, L), 0)
    coli = lax.broadcasted_iota(jnp.int32, (L, L), 1)
    tril = rowi >= coli
    tril_w = jnp.where(tril, 1.0, 0.0).astype(BF16)

    head_lane = lane < SSD_HEADS
    dt = jnp.where(head_lane, _softplus(dt_ref[...] + dtb_ref[...]), 0.0)
    d_a = dt * jnp.where(head_lane[0:1], -jnp.exp(alog_ref[...]) * LOG2E, 0.0)
    cs = sum(_dot(tril_w, part) for part in _split3(d_a))
    cs_t = cs.T
    r = r_ref[...]
    dt_x = _dot3(dt, r)
    cs_x = _dot3(cs, r)
    ecs_x = jnp.exp2(cs_x)
    decay_x = jnp.exp2(cs_x[L - 1:L, :] - cs_x)

    xdt = xs * dt_x
    xdtd = (xdt * decay_x).astype(BF16)
    xdt16 = xdt.astype(BF16)
    y_skip = xs * dexp_ref[...]
    lane_lo = lane < SSD_HEAD_DIM

    hpg = SSD_HEADS // SSD_GROUPS
    gw = hpg * SSD_HEAD_DIM
    y_blocks = []
    for g in range(SSD_GROUPS):
        b_g = xa[:, SSD_INNER + g * SSD_STATE:SSD_INNER + (g + 1) * SSD_STATE]
        c_g = xa[:, SSD_INNER + (SSD_GROUPS + g) * SSD_STATE:SSD_INNER + (SSD_GROUPS + g + 1) * SSD_STATE]
        b16 = b_g.astype(BF16)
        c16 = c_g.astype(BF16)
        cb = _dot_nt(c16, b16)
        st = state[:, g * gw:(g + 1) * gw]
        y_off = _dot(c16, st.astype(BF16)) * ecs_x[:, g * gw:(g + 1) * gw]
        bt16 = b_g.T.astype(BF16)
        state[:, g * gw:(g + 1) * gw] = (st * ecs_x[L - 1:L, g * gw:(g + 1) * gw]
                                         + _dot(bt16, xdtd[:, g * gw:(g + 1) * gw]))
        for hp in range(hpg // 2):
            col = g * gw + hp * LANES
            xp = xdt16[:, col:col + LANES]
            pair = []
            for j in range(2):
                h = g * hpg + hp * 2 + j
                seg = cs[:, h:h + 1] - cs_t[h:h + 1, :]
                lm = jnp.where(tril, jnp.exp2(jnp.where(tril, seg, 0.0)), 0.0)
                pair.append(_dot((cb * lm).astype(BF16), xp))
            y_diag = jnp.where(lane_lo, pair[0], pair[1])
            y_blocks.append(y_diag + y_off[:, hp * LANES:(hp + 1) * LANES] + y_skip[:, col:col + LANES])
    y = jnp.concatenate(y_blocks, axis=1)
    yz = y * _silu(z_ref[...].astype(F32))
    o_ref[...] = _rms_rows(yz, ng_ref[...]).astype(o_ref.dtype)


def _ssd(z, xbc, dt, conv_w, conv_b, dt_bias, a_log, d_skip, norm_g):
    B, S, _ = z.shape
    L = SSD_CHUNK
    HALO = BF16_SUBLANES
    pad = lambda v: jnp.pad(v, (0, LANES - SSD_HEADS)).reshape(1, LANES)
    d_exp = jnp.repeat(d_skip, SSD_HEAD_DIM).reshape(1, SSD_INNER)
    expand = (jnp.arange(LANES)[:, None] == (jnp.arange(SSD_INNER)[None, :] // SSD_HEAD_DIM)).astype(BF16)
    blk = lambda w: pl.BlockSpec((None, L, w), lambda b, c: (b, c, 0))
    halo_spec = pl.BlockSpec((None, HALO, SSD_XBC), lambda b, c: (b, jnp.maximum(c * (L // HALO) - 1, 0), 0))
    return pl.pallas_call(
        _ssd_kernel,
        grid=(B, S // L),
        in_specs=[blk(SSD_INNER), blk(SSD_XBC), halo_spec, blk(LANES),
                  _const_spec((SSD_CONV, SSD_XBC)), _const_spec((1, SSD_XBC)), _const_spec((1, LANES)),
                  _const_spec((1, LANES)), _const_spec((1, SSD_INNER)), _const_spec((1, SSD_INNER)),
                  _const_spec((LANES, SSD_INNER))],
        out_specs=blk(SSD_INNER),
        out_shape=jax.ShapeDtypeStruct((B, S, SSD_INNER), BF16),
        scratch_shapes=[pltpu.VMEM((SSD_STATE, SSD_INNER), F32)],
        compiler_params=_params(("parallel", "arbitrary")),
        name="ssd",
    )(z, xbc, xbc, dt, conv_w, conv_b.reshape(1, SSD_XBC), pad(dt_bias), pad(a_log), d_exp,
      norm_g.reshape(1, SSD_INNER), expand)


def _cmp_kernel(kvc_ref, pea_ref, peb_ref, w1a_ref, w1b_ref, w2_ref, kn_ref, o_ref):
    n_chunk = kvc_ref.shape[0]
    per = NSA_CMP_STRIDE
    row = lax.broadcasted_iota(jnp.int32, (n_chunk, LANES), 0)
    for g in range(NSA_KV_GROUPS):
        x = jnp.concatenate([kvc_ref[:, (NSA_KV_GROUPS * l + g) * LANES:(NSA_KV_GROUPS * l + g + 1) * LANES]
                             for l in range(per)], axis=1).astype(F32)
        a = _dot((x + pea_ref[...]).astype(BF16), w1a_ref[...])
        b = _dot((x + peb_ref[...]).astype(BF16), w1b_ref[...])
        hid = _silu(a + pltpu.roll(b, n_chunk - 1, axis=0))
        cmp = _dot(hid.astype(BF16), w2_ref[...])
        cmp = _head_rms(cmp, kn_ref[...], NSA_HEAD_DIM)
        o_ref[g] = jnp.where(row < n_chunk - 1, cmp, 0.0).astype(o_ref.dtype)


def _cmp(kvc, nsa_cmp_pe, nsa_cmp_w1, nsa_cmp_w2, k_norm0):
    B, S, _ = kvc.shape
    n_chunk = S // NSA_CMP_STRIDE
    per = NSA_CMP_STRIDE
    dk = NSA_HEAD_DIM
    kvc_r = kvc.reshape(B, n_chunk, per * NSA_KV_GROUPS * LANES)
    pe = jnp.concatenate([nsa_cmp_pe[0], nsa_cmp_pe[1]], axis=1)
    pea = pe[:per].reshape(1, per * LANES)
    peb = pe[per:].reshape(1, per * LANES)
    w1 = nsa_cmp_w1.reshape(2, NSA_CMP_LEN, dk, NSA_CMP_HIDDEN)
    zero = jnp.zeros((NSA_CMP_LEN, dk, NSA_CMP_HIDDEN), F32)
    w1bd = jnp.concatenate([jnp.concatenate([w1[0], zero], axis=2), jnp.concatenate([zero, w1[1]], axis=2)], axis=1)
    w1a = w1bd[:per].reshape(per * LANES, 2 * NSA_CMP_HIDDEN).astype(BF16)
    w1b = w1bd[per:].reshape(per * LANES, 2 * NSA_CMP_HIDDEN).astype(BF16)
    z2 = jnp.zeros((NSA_CMP_HIDDEN, dk), F32)
    w2bd = jnp.concatenate([jnp.concatenate([nsa_cmp_w2[0], z2], axis=1),
                            jnp.concatenate([z2, nsa_cmp_w2[1]], axis=1)], axis=0).astype(BF16)
    kn = jnp.concatenate([k_norm0, jnp.ones((LANES - dk,), F32)]).reshape(1, LANES)
    return pl.pallas_call(
        _cmp_kernel,
        grid=(B,),
        in_specs=[pl.BlockSpec((None, n_chunk, per * NSA_KV_GROUPS * LANES), lambda b: (b, 0, 0)),
                  _const_spec(pea.shape), _const_spec(peb.shape), _const_spec(w1a.shape), _const_spec(w1b.shape),
                  _const_spec(w2bd.shape), _const_spec(kn.shape)],
        out_specs=pl.BlockSpec((None, NSA_KV_GROUPS, n_chunk, LANES), lambda b: (b, 0, 0, 0)),
        out_shape=jax.ShapeDtypeStruct((B, NSA_KV_GROUPS, n_chunk, LANES), BF16),
        compiler_params=_params(("parallel",)),
        name="nsa_cmp",
    )(kvc_r, pea, peb, w1a, w1b, w2bd, kn)


def _top_blocks(imp, jt):
    n_slc, TQ = imp.shape
    blk_q = lax.broadcasted_iota(jnp.int32, (n_slc, TQ), 0)
    imp = jnp.where(blk_q == 0, BIG, imp)
    imp = jnp.where(blk_q == jt, BIG, imp)
    imp = jnp.where(blk_q == jt - 1, BIG, imp)
    imp = jnp.where(blk_q > jt, NEG, imp)
    SUB = 8
    row8 = lax.broadcasted_iota(jnp.int32, (SUB, 1), 0)
    groups = [imp[k * SUB:(k + 1) * SUB] for k in range(n_slc // SUB)]
    ranks = [jnp.zeros((SUB, TQ), F32) for _ in groups]
    for i in range(n_slc):
        vi = jnp.broadcast_to(imp[i:i + 1, :], (SUB, TQ))
        for k, grp in enumerate(groups):
            if k * SUB > i:
                ahead = jnp.where(vi >= grp, 1.0, 0.0)
            elif k * SUB + SUB - 1 < i:
                ahead = jnp.where(vi > grp, 1.0, 0.0)
            else:
                ahead = jnp.where(row8 + k * SUB > i, jnp.where(vi >= grp, 1.0, 0.0), jnp.where(vi > grp, 1.0, 0.0))
            ranks[k] = ranks[k] + ahead
    rank = jnp.concatenate(ranks, axis=0)
    return jnp.where(rank < NSA_TOP_N, jnp.where(blk_q <= jt, 1.0, 0.0), 0.0)


def _nsa_kernel(q_ref, kvc_ref, kv_ref, gate_ref, ov_ref, o_ref, k2c, vct, k2s, vst, k2w, vwt, *, seq):
    TQ, TK, W = NSA_TQ, NSA_TK, NSA_WINDOW
    G, HG, DK = NSA_KV_GROUPS, NSA_HG, NSA_HEAD_DIM
    PAIR = 2
    n_cmp_pad = kvc_ref.shape[1]
    n_slc = seq // NSA_SLC_BLOCK
    t0 = pl.program_id(1) * TQ
    slopes = [[2.0 ** -(g * HG + h + 1) for h in range(HG)] for g in range(G)]

    @pl.when(pl.program_id(1) == 0)
    def _build_keys():
        lane = lax.broadcasted_iota(jnp.int32, (LANES, LANES), 1)
        row = lax.broadcasted_iota(jnp.int32, (LANES, LANES), 0)

        def with_pos(kv, pos):
            rel = lane - DK
            feat = jnp.where(rel % 2 == 0, pos // NSA_SLC_BLOCK, pos % NSA_SLC_BLOCK)
            feat = jnp.where(rel < 2 * len(LOG2E_TERMS), feat, 0).astype(F32)
            return jnp.where(rel < 0, kv, feat).astype(BF16)

        for g in range(G):
            for c in range(n_cmp_pad // LANES):
                rows = slice(c * LANES, (c + 1) * LANES)
                kv = kvc_ref[g, rows, :].astype(F32)
                k2c[g, rows, :] = with_pos(kv, (row + c * LANES) * NSA_CMP_STRIDE + (NSA_CMP_LEN - 1))
                vct[g, :, rows] = kv.T[DK:].astype(BF16)

        def chunk(c, carry):
            r0 = pl.multiple_of(c * LANES, LANES)
            pos = row + r0
            one_hot = jnp.where(lane == pos // NSA_SLC_BLOCK, 1.0, 0.0).astype(BF16)
            for g in range(G):
                kv = kv_ref[pl.ds(r0, LANES), 2 * g * LANES:(2 * g + 1) * LANES].astype(F32)
                k2s[g, pl.ds(r0, LANES), 0:LANES] = with_pos(kv, pos)
                k2s[g, pl.ds(r0, LANES), LANES:2 * LANES] = one_hot
                vst[g, :, pl.ds(r0, LANES)] = kv.T[DK:].astype(BF16)
                kvw = kv_ref[pl.ds(r0, LANES), (2 * g + 1) * LANES:(2 * g + 2) * LANES].astype(F32)
                k2w[g, pl.ds(r0, LANES), :] = with_pos(kvw, pos)
                vwt[g, :, pl.ds(r0, LANES)] = kvw.T[DK:].astype(BF16)
            return carry

        lax.fori_loop(0, seq // LANES, chunk, 0)

    rel_q = lax.broadcasted_iota(jnp.int32, (1, LANES), 1) - DK
    q1 = [[None] * HG for _ in range(G)]
    for g in range(G):
        for h in range(HG):
            feat = jnp.zeros((1, LANES), F32)
            for i, term in enumerate(LOG2E_TERMS):
                feat = jnp.where(rel_q == 2 * i, NSA_SLC_BLOCK * slopes[g][h] * term, feat)
                feat = jnp.where(rel_q == 2 * i + 1, slopes[g][h] * term, feat)
            qf = q_ref[:, (g * HG + h) * LANES:(g * HG + h + 1) * LANES].astype(F32) + feat
            q1[g][h] = qf.astype(BF16)
    chains = [(g, hp) for g in range(G) for hp in range(HG // PAIR)]
    q1c = {(g, hp): jnp.concatenate(q1[g][hp * PAIR:(hp + 1) * PAIR], axis=0) for g, hp in chains}
    tq = t0 + lax.broadcasted_iota(jnp.int32, (1, TQ), 1)
    jt = tq // NSA_SLC_BLOCK

    def tile_pair(mask):
        return jnp.concatenate([mask] * PAIR, axis=1)

    cend = lax.broadcasted_iota(jnp.int32, (n_cmp_pad, 1), 0) * NSA_CMP_STRIDE + (NSA_CMP_LEN - 1)
    valid = (tq - cend) >= 0
    o_cmp = {}
    psum = [jnp.zeros((n_cmp_pad, TQ), F32) for _ in range(G)]
    sc_all = [_dot_nt(k2c[g], q1c[g, hp]) for g, hp in chains]
    band = W + TQ
    w0 = pl.multiple_of(jnp.maximum(t0 - W, 0), TQ)
    dw = tq - (w0 + lax.broadcasted_iota(jnp.int32, (band, 1), 0))
    in_win = tile_pair(jnp.where(dw >= 0, jnp.where(dw < W, 0.0, NEG), NEG))
    sw_all = [_dot_nt(k2w[g, pl.ds(w0, band), :], q1c[g, hp]) for g, hp in chains]
    for (g, hp), sct in zip(chains, sc_all):
        p_list = []
        for j in range(PAIR):
            s = jnp.where(valid, sct[:, j * TQ:(j + 1) * TQ], NEG)
            m = jnp.max(s, axis=0, keepdims=True)
            p = jnp.where(valid, jnp.exp2(s - m), 0.0)
            l = jnp.sum(p, axis=0, keepdims=True)
            p = p / jnp.where(l > 0.0, l, 1.0)
            psum[g] = psum[g] + p
            p_list.append(p.astype(BF16))
        o_cmp[g, hp] = _dot(vct[g], jnp.concatenate(p_list, axis=1))
    q2c = {}
    for g in range(G):
        imp = sum(_dot(ov_ref[...], part) for part in _split3(psum[g]))
        sel = _top_blocks(imp[0:n_slc], jt)
        if n_slc < LANES:
            sel = jnp.concatenate([sel, jnp.zeros((LANES - n_slc, TQ), F32)], axis=0)
        sel_neg = jnp.where(sel.T > 0.5, 0.0, -(2.0 ** 100)).astype(BF16)
        for hp in range(HG // PAIR):
            q2c[g, hp] = jnp.concatenate([jnp.concatenate([q1[g][hp * PAIR + j], sel_neg], axis=1)
                                          for j in range(PAIR)], axis=0)

    def slc_step(k0, carry, causal_add):
        out = []
        scores = [_dot_nt(k2s[g, pl.ds(k0, TK), :], q2c[g, hp]) for g, hp in chains]
        for (g, hp), (m, l, acc), s in zip(chains, carry, scores):
            if causal_add is not None:
                s = s + causal_add
            m_new = jnp.maximum(m, jnp.max(s, axis=0, keepdims=True))
            alpha = jnp.exp2(m - m_new)
            p = jnp.exp2(s - m_new)
            l = alpha * l + jnp.sum(p, axis=0, keepdims=True)
            acc = alpha * acc + _dot(vst[g, :, pl.ds(k0, TK)], p.astype(BF16))
            out.append((m_new, l, acc))
        return tuple(out)

    n_past = t0 // TK
    init = tuple((jnp.full((1, PAIR * TQ), NEG, F32), jnp.zeros((1, PAIR * TQ), F32),
                  jnp.zeros((DK, PAIR * TQ), F32)) for _ in chains)
    carry = lax.fori_loop(0, n_past, lambda kt, c: slc_step(pl.multiple_of(kt * TK, TK), c, None), init)
    kd = pl.multiple_of(n_past * TK, TK)
    kpos = kd + lax.broadcasted_iota(jnp.int32, (TK, 1), 0)
    final = slc_step(kd, carry, tile_pair(jnp.where(kpos <= tq, 0.0, NEG)))
    o_slc = {ch: acc / l for ch, (_, l, acc) in zip(chains, final)}

    o_win = {}
    for (g, hp), sw in zip(chains, sw_all):
        sw = sw + in_win
        pw = jnp.exp2(sw - jnp.max(sw, axis=0, keepdims=True))
        o_win[g, hp] = _dot(vwt[g, :, pl.ds(w0, band)], pw.astype(BF16)) / jnp.sum(pw, axis=0, keepdims=True)

    comb = []
    for g in range(G):
        gt = jax.nn.sigmoid(gate_ref[:, g * LANES:(g + 1) * LANES]).T
        for h in range(HG):
            ch = (g, h // PAIR)
            cols = slice((h % PAIR) * TQ, (h % PAIR + 1) * TQ)
            comb.append(gt[h:h + 1] * o_cmp[ch][:, cols] + gt[HG + h:HG + h + 1] * o_slc[ch][:, cols]
                        + gt[2 * HG + h:2 * HG + h + 1] * o_win[ch][:, cols])
    o_ref[...] = jnp.concatenate(comb, axis=0).T.astype(o_ref.dtype)


def _nsa(q, kvcmp, kv, gates, B, S):
    TQ = NSA_TQ
    G = NSA_KV_GROUPS
    assert S % (NSA_CMP_STRIDE * LANES) == 0 and S % NSA_TK == 0 and NSA_TK % TQ == 0 and S >= NSA_WINDOW + TQ
    n_cmp_pad = S // NSA_CMP_STRIDE
    n_slc = S // NSA_SLC_BLOCK
    cs = jnp.arange(n_cmp_pad) * NSA_CMP_STRIDE
    ce = cs + NSA_CMP_LEN - 1
    ss = jnp.arange(LANES) * NSA_SLC_BLOCK
    overlap = jnp.clip(jnp.minimum(ce[None, :], ss[:, None] + NSA_SLC_BLOCK - 1)
                       - jnp.maximum(cs[None, :], ss[:, None]) + 1, 0).astype(F32) / NSA_CMP_LEN
    valid_pair = (jnp.arange(LANES)[:, None] < n_slc) & (jnp.arange(n_cmp_pad)[None, :] < n_cmp_pad - 1)
    ov_t = jnp.where(valid_pair, overlap, 0.0).astype(BF16)
    q3 = q.reshape(B, S, NSA_HEADS * LANES)
    kv3 = kv.reshape(B, S, G * 2 * LANES)
    g3 = gates.reshape(B, S, G * LANES)
    return pl.pallas_call(
        functools.partial(_nsa_kernel, seq=S),
        grid=(B, S // TQ),
        in_specs=[pl.BlockSpec((None, TQ, NSA_HEADS * LANES), lambda b, i: (b, i, 0)),
                  pl.BlockSpec((None, G, n_cmp_pad, LANES), lambda b, i: (b, 0, 0, 0)),
                  pl.BlockSpec((None, S, G * 2 * LANES), lambda b, i: (b, 0, 0)),
                  pl.BlockSpec((None, TQ, G * LANES), lambda b, i: (b, i, 0)),
                  _const_spec(ov_t.shape)],
        out_specs=pl.BlockSpec((None, TQ, NSA_INNER), lambda b, i: (b, i, 0)),
        out_shape=jax.ShapeDtypeStruct((B, S, NSA_INNER), BF16),
        scratch_shapes=[pltpu.VMEM((G, n_cmp_pad, LANES), BF16), pltpu.VMEM((G, NSA_HEAD_DIM, n_cmp_pad), BF16),
                        pltpu.VMEM((G, S, 2 * LANES), BF16), pltpu.VMEM((G, NSA_HEAD_DIM, S), BF16),
                        pltpu.VMEM((G, S, LANES), BF16), pltpu.VMEM((G, NSA_HEAD_DIM, S), BF16)],
        compiler_params=_params(("parallel", "arbitrary")),
        name="nsa_attn",
    )(q3, kvcmp, kv3, g3, ov_t)


def _mem_kv_kernel(mem_ref, g_ref, w_ref, kn_ref, k_o, v_o):
    mn = _rms_rows(mem_ref[...], g_ref[...]).astype(BF16)
    kv = _dot(mn, w_ref[...])
    for h in range(MEM_HEADS):
        sl = slice(h * LANES, (h + 1) * LANES)
        k_o[:, sl] = _head_rms(kv[:, sl], kn_ref[...], MEM_HEAD_DIM).astype(k_o.dtype)
    v_o[...] = kv[:, MEM_INNER:].astype(v_o.dtype)


def _mem_kv(mem, mem_norm, w_mem_kv, mem_k_norm):
    B, M, _ = mem.shape
    spec_o = pl.BlockSpec((None, M, MEM_INNER), lambda b: (b, 0, 0))
    return pl.pallas_call(
        _mem_kv_kernel,
        grid=(B,),
        in_specs=[pl.BlockSpec((None, M, D_MODEL), lambda b: (b, 0, 0)), _const_spec((1, D_MODEL)),
                  _const_spec((D_MODEL, 2 * MEM_INNER)), _const_spec((1, LANES))],
        out_specs=[spec_o, spec_o],
        out_shape=[jax.ShapeDtypeStruct((B, M, MEM_INNER), BF16)] * 2,
        compiler_params=_params(("parallel",)),
        name="mem_kv",
    )(mem, mem_norm.reshape(1, D_MODEL), w_mem_kv.astype(BF16), mem_k_norm.reshape(1, LANES))


def _merge_kernel(x_ref, ssd_ref, nsa_ref, qm_ref, gm_ref, km_ref, vm_ref, wso, wno, wmo, wout, h_o):
    heads = [slice(h * LANES, (h + 1) * LANES) for h in range(MEM_HEADS)]
    scores = [_dot_nt(qm_ref[:, sl], km_ref[:, sl]) for sl in heads]
    y_ssd = _dot(ssd_ref[...], wso[...])
    y_nsa = _dot(nsa_ref[...], wno[...])
    o_mem = []
    for sl, s in zip(heads, scores):
        p = jnp.exp2(s - jnp.max(s, axis=-1, keepdims=True))
        o = _dot(p.astype(BF16), vm_ref[:, sl]) / jnp.sum(p, axis=-1, keepdims=True)
        o_mem.append(o.astype(BF16))
    y_mem = _dot(jnp.concatenate(o_mem, axis=1), wmo[...])
    gate = lambda j: jax.nn.sigmoid(gm_ref[:, j * D_MODEL:(j + 1) * D_MODEL].astype(F32))
    mixed = gate(0) * y_ssd + gate(1) * y_nsa + gate(2) * y_mem
    h_o[...] = x_ref[...] + _dot(mixed.astype(BF16), wout[...])


def _merge(x2, ssd_n, nsa_o, q_mem, g_merge, k_mem, v_mem, w_ssd_o, w_nsa_o, w_mem_o, w_out, S):
    T = x2.shape[0]
    TM = ROW_TILE
    M = k_mem.shape[1]
    per_batch = S // TM
    row = lambda w: pl.BlockSpec((TM, w), lambda i: (i, 0))
    mem_spec = pl.BlockSpec((None, M, MEM_INNER), lambda i: (i // per_batch, 0, 0))
    weights = [w.astype(BF16) for w in (w_ssd_o, w_nsa_o, w_mem_o, w_out)]
    return pl.pallas_call(
        _merge_kernel,
        grid=(T // TM,),
        in_specs=[row(D_MODEL), row(SSD_INNER), row(NSA_INNER), row(MEM_INNER), row(N_BRANCH * D_MODEL),
                  mem_spec, mem_spec] + [_const_spec(w.shape) for w in weights],
        out_specs=row(D_MODEL),
        out_shape=jax.ShapeDtypeStruct((T, D_MODEL), F32),
        compiler_params=_params(("parallel",)),
        name="merge",
    )(x2, ssd_n, nsa_o, q_mem, g_merge, k_mem, v_mem, *weights)


def _ffn_kernel(h_ref, halo_ref, g_ref, wup, cw_ref, cb_ref, wdown, o_ref, *, per_batch):
    TM = h_ref.shape[0]
    HALO = halo_ref.shape[0]
    CW = FFN_CHUNK
    first = pl.program_id(0) % per_batch == 0
    h = h_ref[...]
    halo_n = _rms_rows(halo_ref[...], g_ref[...]) * jnp.where(first, 0.0, 1.0)
    hn = jnp.concatenate([halo_n, _rms_rows(h, g_ref[...])], axis=0).astype(BF16)
    n_chunks = FFN_HIDDEN // CW

    def up_proj(c):
        return [_dot(hn, wup[:, half * FFN_HIDDEN + c * CW:half * FFN_HIDDEN + (c + 1) * CW]) for half in range(2)]

    acc = jnp.zeros((TM, D_MODEL), F32)
    ext = up_proj(0)
    pending = []
    for c in range(n_chunks):
        ext_next = up_proj(c + 1) if c + 1 < n_chunks else None
        conv = []
        for half in range(2):
            col = half * FFN_HIDDEN + c * CW
            u = jnp.broadcast_to(cb_ref[:, col:col + CW], (TM, CW))
            for k in range(FFN_CONV):
                start = HALO - (FFN_CONV - 1) + k
                u = u + cw_ref[k:k + 1, col:col + CW] * ext[half][start:start + TM]
            conv.append(u)
        pending.append((_silu(conv[0]) * conv[1]).astype(BF16))
        if len(pending) == FFN_DOWN_GROUP or c + 1 == n_chunks:
            lo = (c + 1 - len(pending)) * CW
            acc = acc + _dot(jnp.concatenate(pending, axis=1), wdown[lo:(c + 1) * CW, :])
            pending = []
        ext = ext_next
    o_ref[...] = h + acc


def _ffn(h2, norm_ffn, w_ffn_up, ffn_conv_w, ffn_conv_b, w_ffn_down, S):
    T = h2.shape[0]
    TM = ROW_TILE
    HALO = BF16_SUBLANES
    per_batch = S // TM
    return pl.pallas_call(
        functools.partial(_ffn_kernel, per_batch=per_batch),
        grid=(T // TM,),
        in_specs=[pl.BlockSpec((TM, D_MODEL), lambda i: (i, 0)),
                  pl.BlockSpec((HALO, D_MODEL), lambda i: (jnp.maximum(i * (TM // HALO) - 1, 0), 0)),
                  _const_spec((1, D_MODEL)), _const_spec((D_MODEL, 2 * FFN_HIDDEN)),
                  _const_spec((FFN_CONV, 2 * FFN_HIDDEN)), _const_spec((1, 2 * FFN_HIDDEN)),
                  _const_spec((FFN_HIDDEN, D_MODEL))],
        out_specs=pl.BlockSpec((TM, D_MODEL), lambda i: (i, 0)),
        out_shape=jax.ShapeDtypeStruct((T, D_MODEL), F32),
        compiler_params=_params(("parallel",)),
        name="ffn",
    )(h2, h2, norm_ffn.reshape(1, D_MODEL), w_ffn_up.astype(BF16), ffn_conv_w,
      ffn_conv_b.reshape(1, 2 * FFN_HIDDEN), w_ffn_down.astype(BF16))


def _layer(x, mem, norm_mix, w_in, ssd_conv_w, ssd_conv_b, ssd_dt_bias, ssd_a_log, ssd_d, ssd_norm, w_ssd_o,
           nsa_q_norm, nsa_k_norm, nsa_cmp_pe, nsa_cmp_w1, nsa_cmp_w2, w_nsa_o, mem_norm, w_mem_kv,
           mem_q_norm, mem_k_norm, w_mem_o, w_out, norm_ffn, w_ffn_up, ffn_conv_w, ffn_conv_b, w_ffn_down):
    B, S, D = x.shape
    x2 = x.reshape(B * S, D)
    z, xbc, dt, q, kvc, kv, gn, qm, gm = _in_proj(x2, norm_mix, w_in, nsa_q_norm, nsa_k_norm, mem_q_norm)
    ssd_n = _ssd(z.reshape(B, S, -1), xbc.reshape(B, S, -1), dt.reshape(B, S, -1), ssd_conv_w, ssd_conv_b,
                 ssd_dt_bias, ssd_a_log, ssd_d, ssd_norm)
    kvcmp = _cmp(kvc.reshape(B, S, -1), nsa_cmp_pe, nsa_cmp_w1, nsa_cmp_w2, nsa_k_norm[0])
    nsa_o = _nsa(q, kvcmp, kv, gn, B, S)
    k_mem, v_mem = _mem_kv(mem, mem_norm, w_mem_kv, mem_k_norm)
    h2 = _merge(x2, ssd_n.reshape(B * S, -1), nsa_o.reshape(B * S, -1), qm, gm, k_mem, v_mem,
                w_ssd_o, w_nsa_o, w_mem_o, w_out, S)
    out = _ffn(h2, norm_ffn, w_ffn_up, ffn_conv_w, ffn_conv_b, w_ffn_down, S)
    return out.reshape(B, S, D)


def kernel(x, mem, norm_mix, w_in, ssd_conv_w, ssd_conv_b, ssd_dt_bias, ssd_a_log, ssd_d, ssd_norm, w_ssd_o, nsa_q_norm, nsa_k_norm, nsa_cmp_pe, nsa_cmp_w1, nsa_cmp_w2, w_nsa_o, mem_norm, w_mem_kv, mem_q_norm, mem_k_norm, w_mem_o, w_out, norm_ffn, w_ffn_up, ffn_conv_w, ffn_conv_b, w_ffn_down):
    h = x
    for i in range(norm_mix.shape[0]):
        h = _layer(h, mem, norm_mix[i], w_in[i], ssd_conv_w[i], ssd_conv_b[i], ssd_dt_bias[i], ssd_a_log[i],
                   ssd_d[i], ssd_norm[i], w_ssd_o[i], nsa_q_norm[i], nsa_k_norm[i], nsa_cmp_pe[i],
                   nsa_cmp_w1[i], nsa_cmp_w2[i], w_nsa_o[i], mem_norm[i], w_mem_kv[i], mem_q_norm[i],
                   mem_k_norm[i], w_mem_o[i], w_out[i], norm_ffn[i], w_ffn_up[i], ffn_conv_w[i],
                   ffn_conv_b[i], w_ffn_down[i])
    return h
```

```python
import functools

import jax
import jax.numpy as jnp
import numpy as np
from jax import lax
from jax.experimental import pallas as pl
from jax.experimental.pallas import tpu as pltpu

F32 = jnp.float32
BF16 = jnp.bfloat16

D_MODEL = 1024
SSD_HEADS = 16
SSD_HEAD_DIM = 64
SSD_GROUPS = 2
SSD_STATE = 128
SSD_CONV = 4
SSD_CHUNK = 128
SSD_INNER = SSD_HEADS * SSD_HEAD_DIM
SSD_XBC = SSD_INNER + 2 * SSD_GROUPS * SSD_STATE
NSA_HEADS = 8
NSA_KV_GROUPS = 2
NSA_HG = NSA_HEADS // NSA_KV_GROUPS
NSA_HEAD_DIM = 64
NSA_CMP_LEN = 32
NSA_CMP_STRIDE = 16
NSA_CMP_HIDDEN = 256
NSA_SLC_BLOCK = 64
NSA_TOP_N = 16
NSA_WINDOW = 512
NSA_INNER = NSA_HEADS * NSA_HEAD_DIM
NSA_KV = 3 * 2 * NSA_KV_GROUPS * NSA_HEAD_DIM
NSA_GATES = 3 * NSA_HEADS
MEM_HEADS = 4
MEM_HEAD_DIM = 128
MEM_INNER = MEM_HEADS * MEM_HEAD_DIM
N_BRANCH = 3
FFN_HIDDEN = 2816
FFN_CONV = 3
NORM_EPS = 1e-6
NEG = -1e30
BIG = 1e9

IN_SIZES = (SSD_INNER, SSD_XBC, SSD_HEADS, NSA_INNER, NSA_KV, NSA_GATES, MEM_INNER, N_BRANCH * D_MODEL)

LANES = 128
BF16_SUBLANES = 16
VMEM_LIMIT = 56 * 1024 * 1024


def _bf16_terms(x, n):
    terms, rest = [], np.float32(x)
    for _ in range(n):
        t = np.float32(np.asarray(rest).astype(jnp.bfloat16))
        terms.append(float(t))
        rest = np.float32(rest - t)
    return tuple(terms)


LOG2E = float(np.float32(np.log2(np.e)))
LOG2E_TERMS = _bf16_terms(LOG2E, 3)

ROW_TILE = 512
NSA_TQ = 256
NSA_TK = 512
SSD_CHUNKS_PER_STEP = 4
FFN_CHUNK = 256
FFN_DOWN_GROUP = 6


def _dot(a, b):
    return jnp.dot(a, b, preferred_element_type=F32)


def _dot_nt(a, b):
    return lax.dot_general(a, b, (((1,), (1,)), ((), ())), preferred_element_type=F32)


def _split3(x):
    hi = x.astype(BF16)
    r1 = x - hi.astype(F32)
    mid = r1.astype(BF16)
    lo = (r1 - mid.astype(F32)).astype(BF16)
    return hi, mid, lo


def _dot3(x, w):
    hi, mid, lo = _split3(x)
    return _dot(hi, w) + _dot(mid, w) + _dot(lo, w)


def _silu(x):
    return x * jax.nn.sigmoid(x)


def _rms_rows(x, gain):
    return x * lax.rsqrt(jnp.mean(x * x, axis=-1, keepdims=True) + NORM_EPS) * gain


def _const_spec(shape):
    nd = len(shape)
    return pl.BlockSpec(shape, lambda *_: (0,) * nd, pipeline_mode=pl.Buffered(1))


def _params(sem, flags=None):
    return pltpu.CompilerParams(dimension_semantics=sem, vmem_limit_bytes=VMEM_LIMIT, flags=flags)


def _head_rms(v, gain, d):
    if d == LANES:
        ss = jnp.sum(v * v, axis=-1, keepdims=True)
        return v * lax.rsqrt(ss / d + NORM_EPS) * gain
    lane = lax.broadcasted_iota(jnp.int32, v.shape, 1)
    head = lane < d
    ss = jnp.sum(jnp.where(head, v * v, 0.0), axis=-1, keepdims=True)
    return jnp.where(head, v * lax.rsqrt(ss / d + NORM_EPS) * gain, v)


def _pair_rms(v, gain, d):
    lo = lax.broadcasted_iota(jnp.int32, v.shape, 1) < d
    sq = v * v
    ss_lo = jnp.sum(jnp.where(lo, sq, 0.0), axis=-1, keepdims=True)
    ss_hi = jnp.sum(jnp.where(lo, 0.0, sq), axis=-1, keepdims=True)
    return v * jnp.where(lo, lax.rsqrt(ss_lo / d + NORM_EPS), lax.rsqrt(ss_hi / d + NORM_EPS)) * gain


def _in_proj_kernel(x_ref, g_ref, wz, wxbc, wsm, wq, wkvc, wkv, wqm, wgm, qn_ref, kn_ref, mqn_ref,
                    z_o, xbc_o, sm_o, q_o, kvc_o, kv_o, qm_o, gm_o):
    xn = _rms_rows(x_ref[...], g_ref[...]).astype(BF16)
    z_o[...] = _dot(xn, wz[...]).astype(z_o.dtype)
    xbc_o[...] = _dot(xn, wxbc[...]).astype(xbc_o.dtype)
    sm_o[...] = _dot(xn, wsm[...])
    gm_o[...] = _dot(xn, wgm[...]).astype(gm_o.dtype)
    kvc_o[...] = _dot(xn, wkvc[...]).astype(kvc_o.dtype)
    q = _dot(xn, wq[...])
    for p in range(NSA_HEADS // 2):
        sl = slice(p * LANES, (p + 1) * LANES)
        q_o[:, sl] = _pair_rms(q[:, sl], qn_ref[...], NSA_HEAD_DIM).astype(q_o.dtype)
    kv = _dot(xn, wkv[...])
    for j in range(2 * NSA_KV_GROUPS):
        sl = slice(j * LANES, (j + 1) * LANES)
        kv_o[:, sl] = _head_rms(kv[:, sl], kn_ref[j % 2:j % 2 + 1, :], NSA_HEAD_DIM).astype(kv_o.dtype)
    qm = _dot(xn, wqm[...])
    for h in range(MEM_HEADS):
        sl = slice(h * LANES, (h + 1) * LANES)
        qm_o[:, sl] = _head_rms(qm[:, sl], mqn_ref[...], MEM_HEAD_DIM).astype(qm_o.dtype)


def _in_proj(x2, norm_mix, w_in, nsa_q_norm, nsa_k_norm, mem_q_norm):
    T = x2.shape[0]
    TM = ROW_TILE
    o = [0]
    for s in IN_SIZES:
        o.append(o[-1] + s)
    wz = w_in[:, o[0]:o[1]]
    wxbc = w_in[:, o[1]:o[2]]
    wq = w_in[:, o[3]:o[4]]
    wkv5 = w_in[:, o[4]:o[5]].reshape(D_MODEL, 3, 2, NSA_KV_GROUPS, NSA_HEAD_DIM)
    wkvc = wkv5[:, 0].transpose(0, 2, 1, 3).reshape(D_MODEL, NSA_KV_GROUPS * LANES)
    wkv = wkv5[:, 1:].transpose(0, 3, 1, 2, 4).reshape(D_MODEL, NSA_KV_GROUPS * 2 * LANES)
    wgn = w_in[:, o[5]:o[6]].reshape(D_MODEL, 3, NSA_KV_GROUPS, NSA_HG).transpose(0, 2, 1, 3).reshape(D_MODEL, NSA_GATES)
    wsm = jnp.pad(jnp.concatenate([w_in[:, o[2]:o[3]], wgn], axis=1), ((0, 0), (0, LANES - SSD_HEADS - NSA_GATES)))
    wqm = w_in[:, o[6]:o[7]]
    wgm = w_in[:, o[7]:o[8]]
    weights = [w.astype(BF16) for w in (wz, wxbc, wsm, wq, wkvc, wkv, wqm, wgm)]
    qn = jnp.tile(nsa_q_norm * (NSA_HEAD_DIM ** -0.5 * LOG2E), 2).reshape(1, LANES)
    kn = jnp.concatenate([nsa_k_norm[1:3], jnp.ones((2, LANES - NSA_HEAD_DIM), F32)], axis=1)
    mqn = (mem_q_norm * (MEM_HEAD_DIM ** -0.5 * LOG2E)).reshape(1, LANES)
    widths = [w.shape[1] for w in weights]
    dtypes = [BF16, BF16, F32, BF16, BF16, BF16, BF16, BF16]
    out_shape = [jax.ShapeDtypeStruct((T, w), dt) for w, dt in zip(widths, dtypes)]
    row = lambda w: pl.BlockSpec((TM, w), lambda i: (i, 0))
    return pl.pallas_call(
        _in_proj_kernel,
        grid=(T // TM,),
        in_specs=[row(D_MODEL), _const_spec((1, D_MODEL))] + [_const_spec(w.shape) for w in weights]
        + [_const_spec((1, LANES)), _const_spec((2, LANES)), _const_spec((1, LANES))],
        out_specs=[row(w) for w in widths],
        out_shape=out_shape,
        compiler_params=_params(("parallel",)),
        name="in_proj",
    )(x2, norm_mix.reshape(1, D_MODEL), *weights, qn, kn, mqn)


def _softplus(x):
    return jnp.maximum(x, 0.0) + jnp.log1p(jnp.exp(-jnp.abs(x)))


def _ssd_kernel(z_ref, xbc_ref, halo_ref, dt_ref, cw_ref, cb_ref, dtb_ref, alog_ref, dexp_ref, ng_ref, r_ref,
                o_ref, state):
    L = SSD_CHUNK
    HALO = BF16_SUBLANES
    c = pl.program_id(1)

    @pl.when(c == 0)
    def _():
        state[...] = jnp.zeros_like(state)

    for sc in range(z_ref.shape[0] // L):
        rows = slice(sc * L, (sc + 1) * L)
        if sc == 0:
            halo = jnp.where(c == 0, jnp.zeros_like(halo_ref[...]), halo_ref[...])
        else:
            halo = xbc_ref[sc * L - HALO:sc * L, :]
        o_ref[rows, :] = _ssd_chunk(z_ref[rows, :], xbc_ref[rows, :], halo, dt_ref[rows, :], cw_ref, cb_ref, dtb_ref,
                                    alog_ref, dexp_ref, ng_ref, r_ref, state).astype(o_ref.dtype)


def _ssd_chunk(z, xbc, halo, dt_raw, cw_ref, cb_ref, dtb_ref, alog_ref, dexp_ref, ng_ref, r_ref, state):
    L = SSD_CHUNK
    HALO = BF16_SUBLANES
    xe = jnp.concatenate([halo, xbc], axis=0)
    src = lax.broadcasted_iota(jnp.int32, (L, HALO + L), 1) - lax.broadcasted_iota(jnp.int32, (L, HALO + L), 0)
    acc = cb_ref[...] + cw_ref[SSD_CONV - 1:SSD_CONV, :] * xbc.astype(F32)
    for k in range(SSD_CONV - 1):
        shift = jnp.where(src == HALO - (SSD_CONV - 1) + k, 1.0, 0.0).astype(BF16)
        acc = acc + cw_ref[k:k + 1, :] * _dot(shift, xe)
    xa = _silu(acc)
    xs = xa[:, :SSD_INNER]

    lane = lax.broadcasted_iota(jnp.int32, (L, LANES), 1)
    rowi = lax.broadcasted_iota(jnp.int32, (L, L), 0)
    coli = lax.broadcasted_iota(jnp.int32, (L, L), 1)
    tril = rowi >= coli
    tril_w = jnp.where(tril, 1.0, 0.0).astype(BF16)

    head_lane = lane < SSD_HEADS
    dt = jnp.where(head_lane, _softplus(dt_raw + dtb_ref[...]), 0.0)
    d_a = dt * jnp.where(head_lane[0:1], -jnp.exp(alog_ref[...]) * LOG2E, 0.0)
    cs = sum(_dot(tril_w, part) for part in _split3(d_a))
    cs_t = cs.T
    r = r_ref[...]
    dt_x = _dot3(dt, r)
    cs_x = _dot3(cs, r)
    ecs_x = jnp.exp2(cs_x)
    decay_x = jnp.exp2(cs_x[L - 1:L, :] - cs_x)

    xdt = xs * dt_x
    xdtd = (xdt * decay_x).astype(BF16)
    xdt16 = xdt.astype(BF16)
    y_skip = xs * dexp_ref[...]
    lane_lo = lane < SSD_HEAD_DIM

    hpg = SSD_HEADS // SSD_GROUPS
    gw = hpg * SSD_HEAD_DIM
    y_blocks = []
    for g in range(SSD_GROUPS):
        b_g = xa[:, SSD_INNER + g * SSD_STATE:SSD_INNER + (g + 1) * SSD_STATE]
        c_g = xa[:, SSD_INNER + (SSD_GROUPS + g) * SSD_STATE:SSD_INNER + (SSD_GROUPS + g + 1) * SSD_STATE]
        b16 = b_g.astype(BF16)
        c16 = c_g.astype(BF16)
        cb = _dot_nt(c16, b16)
        st = state[:, g * gw:(g + 1) * gw]
        y_off = _dot(c16, st.astype(BF16)) * ecs_x[:, g * gw:(g + 1) * gw]
        bt16 = b_g.T.astype(BF16)
        state[:, g * gw:(g + 1) * gw] = (st * ecs_x[L - 1:L, g * gw:(g + 1) * gw]
                                         + _dot(bt16, xdtd[:, g * gw:(g + 1) * gw]))
        for hp in range(hpg // 2):
            col = g * gw + hp * LANES
            xp = xdt16[:, col:col + LANES]
            pair = []
            for j in range(2):
                h = g * hpg + hp * 2 + j
                seg = cs[:, h:h + 1] - cs_t[h:h + 1, :]
                lm = jnp.where(tril, jnp.exp2(jnp.where(tril, seg, 0.0)), 0.0)
                pair.append(_dot((cb * lm).astype(BF16), xp))
            y_diag = jnp.where(lane_lo, pair[0], pair[1])
            y_blocks.append(y_diag + y_off[:, hp * LANES:(hp + 1) * LANES] + y_skip[:, col:col + LANES])
    y = jnp.concatenate(y_blocks, axis=1)
    yz = y * _silu(z.astype(F32))
    return _rms_rows(yz, ng_ref[...])


def _ssd(z, xbc, dt, conv_w, conv_b, dt_bias, a_log, d_skip, norm_g):
    B, S, _ = z.shape
    L = SSD_CHUNK
    HALO = BF16_SUBLANES
    pad = lambda v: jnp.pad(v, (0, LANES - SSD_HEADS)).reshape(1, LANES)
    d_exp = jnp.repeat(d_skip, SSD_HEAD_DIM).reshape(1, SSD_INNER)
    expand = (jnp.arange(LANES)[:, None] == (jnp.arange(SSD_INNER)[None, :] // SSD_HEAD_DIM)).astype(BF16)
    step = SSD_CHUNKS_PER_STEP * SSD_CHUNK
    blk = lambda w: pl.BlockSpec((None, step, w), lambda b, c: (b, c, 0))
    halo_spec = pl.BlockSpec((None, HALO, SSD_XBC), lambda b, c: (b, jnp.maximum(c * (step // HALO) - 1, 0), 0))
    return pl.pallas_call(
        _ssd_kernel,
        grid=(B, S // step),
        in_specs=[blk(SSD_INNER), blk(SSD_XBC), halo_spec, blk(LANES),
                  _const_spec((SSD_CONV, SSD_XBC)), _const_spec((1, SSD_XBC)), _const_spec((1, LANES)),
                  _const_spec((1, LANES)), _const_spec((1, SSD_INNER)), _const_spec((1, SSD_INNER)),
                  _const_spec((LANES, SSD_INNER))],
        out_specs=blk(SSD_INNER),
        out_shape=jax.ShapeDtypeStruct((B, S, SSD_INNER), BF16),
        scratch_shapes=[pltpu.VMEM((SSD_STATE, SSD_INNER), F32)],
        compiler_params=_params(("parallel", "arbitrary")),
        name="ssd",
    )(z, xbc, xbc, dt, conv_w, conv_b.reshape(1, SSD_XBC), pad(dt_bias), pad(a_log), d_exp,
      norm_g.reshape(1, SSD_INNER), expand)


def _cmp_kernel(kvc_ref, pea_ref, peb_ref, w1a_ref, w1b_ref, w2_ref, kn_ref, o_ref):
    n_chunk = kvc_ref.shape[0]
    per = NSA_CMP_STRIDE
    row = lax.broadcasted_iota(jnp.int32, (n_chunk, LANES), 0)
    for g in range(NSA_KV_GROUPS):
        x = jnp.concatenate([kvc_ref[:, (NSA_KV_GROUPS * l + g) * LANES:(NSA_KV_GROUPS * l + g + 1) * LANES]
                             for l in range(per)], axis=1).astype(F32)
        a = _dot((x + pea_ref[...]).astype(BF16), w1a_ref[...])
        b = _dot((x + peb_ref[...]).astype(BF16), w1b_ref[...])
        hid = _silu(a + pltpu.roll(b, n_chunk - 1, axis=0))
        cmp = _dot(hid.astype(BF16), w2_ref[...])
        cmp = _head_rms(cmp, kn_ref[...], NSA_HEAD_DIM)
        o_ref[g] = jnp.where(row < n_chunk - 1, cmp, 0.0).astype(o_ref.dtype)


def _cmp(kvc, nsa_cmp_pe, nsa_cmp_w1, nsa_cmp_w2, k_norm0):
    B, S, _ = kvc.shape
    n_chunk = S // NSA_CMP_STRIDE
    per = NSA_CMP_STRIDE
    dk = NSA_HEAD_DIM
    kvc_r = kvc.reshape(B, n_chunk, per * NSA_KV_GROUPS * LANES)
    pe = jnp.concatenate([nsa_cmp_pe[0], nsa_cmp_pe[1]], axis=1)
    pea = pe[:per].reshape(1, per * LANES)
    peb = pe[per:].reshape(1, per * LANES)
    w1 = nsa_cmp_w1.reshape(2, NSA_CMP_LEN, dk, NSA_CMP_HIDDEN)
    zero = jnp.zeros((NSA_CMP_LEN, dk, NSA_CMP_HIDDEN), F32)
    w1bd = jnp.concatenate([jnp.concatenate([w1[0], zero], axis=2), jnp.concatenate([zero, w1[1]], axis=2)], axis=1)
    w1a = w1bd[:per].reshape(per * LANES, 2 * NSA_CMP_HIDDEN).astype(BF16)
    w1b = w1bd[per:].reshape(per * LANES, 2 * NSA_CMP_HIDDEN).astype(BF16)
    z2 = jnp.zeros((NSA_CMP_HIDDEN, dk), F32)
    w2bd = jnp.concatenate([jnp.concatenate([nsa_cmp_w2[0], z2], axis=1),
                            jnp.concatenate([z2, nsa_cmp_w2[1]], axis=1)], axis=0).astype(BF16)
    kn = jnp.concatenate([k_norm0, jnp.ones((LANES - dk,), F32)]).reshape(1, LANES)
    return pl.pallas_call(
        _cmp_kernel,
        grid=(B,),
        in_specs=[pl.BlockSpec((None, n_chunk, per * NSA_KV_GROUPS * LANES), lambda b: (b, 0, 0)),
                  _const_spec(pea.shape), _const_spec(peb.shape), _const_spec(w1a.shape), _const_spec(w1b.shape),
                  _const_spec(w2bd.shape), _const_spec(kn.shape)],
        out_specs=pl.BlockSpec((None, NSA_KV_GROUPS, n_chunk, LANES), lambda b: (b, 0, 0, 0)),
        out_shape=jax.ShapeDtypeStruct((B, NSA_KV_GROUPS, n_chunk, LANES), BF16),
        compiler_params=_params(("parallel",)),
        name="nsa_cmp",
    )(kvc_r, pea, peb, w1a, w1b, w2bd, kn)


def _top_blocks(imp, jt):
    n_slc, TQ = imp.shape
    blk_q = lax.broadcasted_iota(jnp.int32, (n_slc, TQ), 0)
    imp = jnp.where(blk_q == 0, BIG, imp)
    imp = jnp.where(blk_q == jt, BIG, imp)
    imp = jnp.where(blk_q == jt - 1, BIG, imp)
    imp = jnp.where(blk_q > jt, NEG, imp)
    SUB = 8
    row8 = lax.broadcasted_iota(jnp.int32, (SUB, 1), 0)
    groups = [imp[k * SUB:(k + 1) * SUB] for k in range(n_slc // SUB)]
    ranks = [jnp.zeros((SUB, TQ), F32) for _ in groups]
    for i in range(n_slc):
        vi = jnp.broadcast_to(imp[i:i + 1, :], (SUB, TQ))
        for k, grp in enumerate(groups):
            if k * SUB > i:
                ahead = jnp.where(vi >= grp, 1.0, 0.0)
            elif k * SUB + SUB - 1 < i:
                ahead = jnp.where(vi > grp, 1.0, 0.0)
            else:
                ahead = jnp.where(row8 + k * SUB > i, jnp.where(vi >= grp, 1.0, 0.0), jnp.where(vi > grp, 1.0, 0.0))
            ranks[k] = ranks[k] + ahead
    rank = jnp.concatenate(ranks, axis=0)
    return jnp.where(rank < NSA_TOP_N, jnp.where(blk_q <= jt, 1.0, 0.0), 0.0)


def _nsa_kernel(q_ref, kvc_ref, kv_ref, gate_ref, ov_ref, o_ref, k2c, vct, k2s, vst, k2w, vwt, *, seq):
    TQ, TK, W = NSA_TQ, NSA_TK, NSA_WINDOW
    G, HG, DK = NSA_KV_GROUPS, NSA_HG, NSA_HEAD_DIM
    PAIR = 2
    n_cmp_pad = kvc_ref.shape[1]
    n_slc = seq // NSA_SLC_BLOCK
    t0 = pl.program_id(1) * TQ
    slopes = [[2.0 ** -(g * HG + h + 1) for h in range(HG)] for g in range(G)]

    @pl.when(pl.program_id(1) == 0)
    def _build_keys():
        lane = lax.broadcasted_iota(jnp.int32, (LANES, LANES), 1)
        row = lax.broadcasted_iota(jnp.int32, (LANES, LANES), 0)

        def with_pos(kv, pos):
            rel = lane - DK
            feat = jnp.where(rel % 2 == 0, pos // NSA_SLC_BLOCK, pos % NSA_SLC_BLOCK)
            feat = jnp.where(rel < 2 * len(LOG2E_TERMS), feat, 0).astype(F32)
            return jnp.where(rel < 0, kv, feat).astype(BF16)

        for g in range(G):
            for c in range(n_cmp_pad // LANES):
                rows = slice(c * LANES, (c + 1) * LANES)
                kv = kvc_ref[g, rows, :].astype(F32)
                k2c[g, rows, :] = with_pos(kv, (row + c * LANES) * NSA_CMP_STRIDE + (NSA_CMP_LEN - 1))
                vct[g, :, rows] = kv.T[DK:].astype(BF16)

        def chunk(c, carry):
            r0 = pl.multiple_of(c * LANES, LANES)
            pos = row + r0
            one_hot = jnp.where(lane == pos // NSA_SLC_BLOCK, 1.0, 0.0).astype(BF16)
            for g in range(G):
                kv = kv_ref[pl.ds(r0, LANES), 2 * g * LANES:(2 * g + 1) * LANES].astype(F32)
                k2s[g, pl.ds(r0, LANES), 0:LANES] = with_pos(kv, pos)
                k2s[g, pl.ds(r0, LANES), LANES:2 * LANES] = one_hot
                vst[g, :, pl.ds(r0, LANES)] = kv.T[DK:].astype(BF16)
                kvw = kv_ref[pl.ds(r0, LANES), (2 * g + 1) * LANES:(2 * g + 2) * LANES].astype(F32)
                k2w[g, pl.ds(r0, LANES), :] = with_pos(kvw, pos)
                vwt[g, :, pl.ds(r0, LANES)] = kvw.T[DK:].astype(BF16)
            return carry

        lax.fori_loop(0, seq // LANES, chunk, 0)

    rel_q = lax.broadcasted_iota(jnp.int32, (1, LANES), 1) - DK
    lane_lo = lax.broadcasted_iota(jnp.int32, (TQ, LANES), 1) < DK
    q1 = [[None] * HG for _ in range(G)]
    for g in range(G):
        for h in range(HG):
            feat = jnp.zeros((1, LANES), F32)
            for i, term in enumerate(LOG2E_TERMS):
                feat = jnp.where(rel_q == 2 * i, NSA_SLC_BLOCK * slopes[g][h] * term, feat)
                feat = jnp.where(rel_q == 2 * i + 1, slopes[g][h] * term, feat)
            head = g * HG + h
            qpair = q_ref[:, (head // 2) * LANES:(head // 2 + 1) * LANES].astype(F32)
            if head % 2:
                qpair = pltpu.roll(qpair, DK, axis=1)
            q1[g][h] = (jnp.where(lane_lo, qpair, 0.0) + feat).astype(BF16)
    chains = [(g, hp) for g in range(G) for hp in range(HG // PAIR)]
    q1c = {(g, hp): jnp.concatenate(q1[g][hp * PAIR:(hp + 1) * PAIR], axis=0) for g, hp in chains}
    tq = t0 + lax.broadcasted_iota(jnp.int32, (1, TQ), 1)
    jt = tq // NSA_SLC_BLOCK

    def tile_pair(mask):
        return jnp.concatenate([mask] * PAIR, axis=1)

    cend = lax.broadcasted_iota(jnp.int32, (n_cmp_pad, 1), 0) * NSA_CMP_STRIDE + (NSA_CMP_LEN - 1)
    valid = (tq - cend) >= 0
    o_cmp = {}
    psum = [jnp.zeros((n_cmp_pad, TQ), F32) for _ in range(G)]
    sc_all = [_dot_nt(k2c[g], q1c[g, hp]) for g, hp in chains]
    band = W + TQ
    w0 = pl.multiple_of(jnp.maximum(t0 - W, 0), TQ)
    dw = tq - (w0 + lax.broadcasted_iota(jnp.int32, (band, 1), 0))
    in_win = tile_pair(jnp.where(dw >= 0, jnp.where(dw < W, 0.0, NEG), NEG))
    sw_all = [_dot_nt(k2w[g, pl.ds(w0, band), :], q1c[g, hp]) for g, hp in chains]
    for (g, hp), sct in zip(chains, sc_all):
        p_list = []
        for j in range(PAIR):
            s = jnp.where(valid, sct[:, j * TQ:(j + 1) * TQ], NEG)
            m = jnp.max(s, axis=0, keepdims=True)
            p = jnp.where(valid, jnp.exp2(s - m), 0.0)
            l = jnp.sum(p, axis=0, keepdims=True)
            p = p / jnp.where(l > 0.0, l, 1.0)
            psum[g] = psum[g] + p
            p_list.append(p.astype(BF16))
        o_cmp[g, hp] = _dot(vct[g], jnp.concatenate(p_list, axis=1))
    q2c = {}
    for g in range(G):
        imp = sum(_dot(ov_ref[...], part) for part in _split3(psum[g]))
        sel = _top_blocks(imp[0:n_slc], jt)
        if n_slc < LANES:
            sel = jnp.concatenate([sel, jnp.zeros((LANES - n_slc, TQ), F32)], axis=0)
        sel_neg = jnp.where(sel.T > 0.5, 0.0, -(2.0 ** 100)).astype(BF16)
        for hp in range(HG // PAIR):
            q2c[g, hp] = jnp.concatenate([jnp.concatenate([q1[g][hp * PAIR + j], sel_neg], axis=1)
                                          for j in range(PAIR)], axis=0)

    def slc_step(k0, carry, causal_add):
        out = []
        scores = [_dot_nt(k2s[g, pl.ds(k0, TK), :], q2c[g, hp]) for g, hp in chains]
        for (g, hp), (m, l, acc), s in zip(chains, carry, scores):
            if causal_add is not None:
                s = s + causal_add
            m_new = jnp.maximum(m, jnp.max(s, axis=0, keepdims=True))
            alpha = jnp.exp2(m - m_new)
            p = jnp.exp2(s - m_new)
            l = alpha * l + jnp.sum(p, axis=0, keepdims=True)
            acc = alpha * acc + _dot(vst[g, :, pl.ds(k0, TK)], p.astype(BF16))
            out.append((m_new, l, acc))
        return tuple(out)

    n_past = t0 // TK
    init = tuple((jnp.full((1, PAIR * TQ), NEG, F32), jnp.zeros((1, PAIR * TQ), F32),
                  jnp.zeros((DK, PAIR * TQ), F32)) for _ in chains)
    carry = lax.fori_loop(0, n_past, lambda kt, c: slc_step(pl.multiple_of(kt * TK, TK), c, None), init)
    kd = pl.multiple_of(n_past * TK, TK)
    kpos = kd + lax.broadcasted_iota(jnp.int32, (TK, 1), 0)
    final = slc_step(kd, carry, tile_pair(jnp.where(kpos <= tq, 0.0, NEG)))
    o_slc = {ch: acc / l for ch, (_, l, acc) in zip(chains, final)}

    o_win = {}
    for (g, hp), sw in zip(chains, sw_all):
        sw = sw + in_win
        pw = jnp.exp2(sw - jnp.max(sw, axis=0, keepdims=True))
        o_win[g, hp] = _dot(vwt[g, :, pl.ds(w0, band)], pw.astype(BF16)) / jnp.sum(pw, axis=0, keepdims=True)

    comb = []
    gt = jax.nn.sigmoid(gate_ref[...]).T
    for g in range(G):
        for h in range(HG):
            ch = (g, h // PAIR)
            cols = slice((h % PAIR) * TQ, (h % PAIR + 1) * TQ)
            gate = [gt[r:r + 1] for r in (SSD_HEADS + (g * 3 + br) * HG + h for br in range(3))]
            comb.append(gate[0] * o_cmp[ch][:, cols] + gate[1] * o_slc[ch][:, cols]
                        + gate[2] * o_win[ch][:, cols])
    o_ref[...] = jnp.concatenate(comb, axis=0).T.astype(o_ref.dtype)


def _nsa(q, kvcmp, kv, gates, B, S):
    TQ = NSA_TQ
    G = NSA_KV_GROUPS
    assert S % (NSA_CMP_STRIDE * LANES) == 0 and S % NSA_TK == 0 and NSA_TK % TQ == 0 and S >= NSA_WINDOW + TQ
    n_cmp_pad = S // NSA_CMP_STRIDE
    n_slc = S // NSA_SLC_BLOCK
    cs = jnp.arange(n_cmp_pad) * NSA_CMP_STRIDE
    ce = cs + NSA_CMP_LEN - 1
    ss = jnp.arange(LANES) * NSA_SLC_BLOCK
    overlap = jnp.clip(jnp.minimum(ce[None, :], ss[:, None] + NSA_SLC_BLOCK - 1)
                       - jnp.maximum(cs[None, :], ss[:, None]) + 1, 0).astype(F32) / NSA_CMP_LEN
    valid_pair = (jnp.arange(LANES)[:, None] < n_slc) & (jnp.arange(n_cmp_pad)[None, :] < n_cmp_pad - 1)
    ov_t = jnp.where(valid_pair, overlap, 0.0).astype(BF16)
    q3 = q.reshape(B, S, NSA_INNER)
    kv3 = kv.reshape(B, S, G * 2 * LANES)
    g3 = gates.reshape(B, S, LANES)
    return pl.pallas_call(
        functools.partial(_nsa_kernel, seq=S),
        grid=(B, S // TQ),
        in_specs=[pl.BlockSpec((None, TQ, NSA_INNER), lambda b, i: (b, i, 0)),
                  pl.BlockSpec((None, G, n_cmp_pad, LANES), lambda b, i: (b, 0, 0, 0)),
                  pl.BlockSpec((None, S, G * 2 * LANES), lambda b, i: (b, 0, 0)),
                  pl.BlockSpec((None, TQ, LANES), lambda b, i: (b, i, 0)),
                  _const_spec(ov_t.shape)],
        out_specs=pl.BlockSpec((None, TQ, NSA_INNER), lambda b, i: (b, i, 0)),
        out_shape=jax.ShapeDtypeStruct((B, S, NSA_INNER), BF16),
        scratch_shapes=[pltpu.VMEM((G, n_cmp_pad, LANES), BF16), pltpu.VMEM((G, NSA_HEAD_DIM, n_cmp_pad), BF16),
                        pltpu.VMEM((G, S, 2 * LANES), BF16), pltpu.VMEM((G, NSA_HEAD_DIM, S), BF16),
                        pltpu.VMEM((G, S, LANES), BF16), pltpu.VMEM((G, NSA_HEAD_DIM, S), BF16)],
        compiler_params=_params(("parallel", "arbitrary")),
        name="nsa_attn",
    )(q3, kvcmp, kv3, g3, ov_t)


def _mem_kv_kernel(mem_ref, g_ref, w_ref, kn_ref, k_o, v_o):
    mn = _rms_rows(mem_ref[...], g_ref[...]).astype(BF16)
    kv = _dot(mn, w_ref[...])
    for h in range(MEM_HEADS):
        sl = slice(h * LANES, (h + 1) * LANES)
        k_o[:, sl] = _head_rms(kv[:, sl], kn_ref[...], MEM_HEAD_DIM).astype(k_o.dtype)
    v_o[...] = kv[:, MEM_INNER:].astype(v_o.dtype)


def _mem_kv(mem, mem_norm, w_mem_kv, mem_k_norm):
    B, M, _ = mem.shape
    spec_o = pl.BlockSpec((None, M, MEM_INNER), lambda b: (b, 0, 0))
    return pl.pallas_call(
        _mem_kv_kernel,
        grid=(B,),
        in_specs=[pl.BlockSpec((None, M, D_MODEL), lambda b: (b, 0, 0)), _const_spec((1, D_MODEL)),
                  _const_spec((D_MODEL, 2 * MEM_INNER)), _const_spec((1, LANES))],
        out_specs=[spec_o, spec_o],
        out_shape=[jax.ShapeDtypeStruct((B, M, MEM_INNER), BF16)] * 2,
        compiler_params=_params(("parallel",)),
        name="mem_kv",
    )(mem, mem_norm.reshape(1, D_MODEL), w_mem_kv.astype(BF16), mem_k_norm.reshape(1, LANES))


def _merge_kernel(x_ref, ssd_ref, nsa_ref, qm_ref, gm_ref, km_ref, vm_ref, wso, wno, wmo, wout, h_o):
    heads = [slice(h * LANES, (h + 1) * LANES) for h in range(MEM_HEADS)]
    scores = [_dot_nt(qm_ref[:, sl], km_ref[:, sl]) for sl in heads]
    y_ssd = _dot(ssd_ref[...], wso[...])
    y_nsa = _dot(nsa_ref[...], wno[...])
    o_mem = []
    for sl, s in zip(heads, scores):
        p = jnp.exp2(s - jnp.max(s, axis=-1, keepdims=True))
        o = _dot(p.astype(BF16), vm_ref[:, sl]) / jnp.sum(p, axis=-1, keepdims=True)
        o_mem.append(o.astype(BF16))
    y_mem = _dot(jnp.concatenate(o_mem, axis=1), wmo[...])
    gate = lambda j: jax.nn.sigmoid(gm_ref[:, j * D_MODEL:(j + 1) * D_MODEL].astype(F32))
    mixed = gate(0) * y_ssd + gate(1) * y_nsa + gate(2) * y_mem
    h_o[...] = x_ref[...] + _dot(mixed.astype(BF16), wout[...])


def _merge(x2, ssd_n, nsa_o, q_mem, g_merge, k_mem, v_mem, w_ssd_o, w_nsa_o, w_mem_o, w_out, S):
    T = x2.shape[0]
    TM = ROW_TILE
    M = k_mem.shape[1]
    per_batch = S // TM
    row = lambda w: pl.BlockSpec((TM, w), lambda i: (i, 0))
    mem_spec = pl.BlockSpec((None, M, MEM_INNER), lambda i: (i // per_batch, 0, 0))
    weights = [w.astype(BF16) for w in (w_ssd_o, w_nsa_o, w_mem_o, w_out)]
    return pl.pallas_call(
        _merge_kernel,
        grid=(T // TM,),
        in_specs=[row(D_MODEL), row(SSD_INNER), row(NSA_INNER), row(MEM_INNER), row(N_BRANCH * D_MODEL),
                  mem_spec, mem_spec] + [_const_spec(w.shape) for w in weights],
        out_specs=row(D_MODEL),
        out_shape=jax.ShapeDtypeStruct((T, D_MODEL), F32),
        compiler_params=_params(("parallel",)),
        name="merge",
    )(x2, ssd_n, nsa_o, q_mem, g_merge, k_mem, v_mem, *weights)


def _ffn_kernel(h_ref, halo_ref, g_ref, wup, cw_ref, cb_ref, wdown, o_ref, *, per_batch):
    TM = h_ref.shape[0]
    HALO = halo_ref.shape[0]
    CW = FFN_CHUNK
    first = pl.program_id(0) % per_batch == 0
    h = h_ref[...]
    halo_n = _rms_rows(halo_ref[...], g_ref[...]) * jnp.where(first, 0.0, 1.0)
    hn = jnp.concatenate([halo_n, _rms_rows(h, g_ref[...])], axis=0).astype(BF16)
    n_chunks = FFN_HIDDEN // CW

    def up_proj(c):
        return [_dot(hn, wup[:, half * FFN_HIDDEN + c * CW:half * FFN_HIDDEN + (c + 1) * CW]) for half in range(2)]

    acc = jnp.zeros((TM, D_MODEL), F32)
    ext = up_proj(0)
    pending = []
    for c in range(n_chunks):
        ext_next = up_proj(c + 1) if c + 1 < n_chunks else None
        conv = []
        for half in range(2):
            col = half * FFN_HIDDEN + c * CW
            u = jnp.broadcast_to(cb_ref[:, col:col + CW], (TM, CW))
            for k in range(FFN_CONV):
                start = HALO - (FFN_CONV - 1) + k
                u = u + cw_ref[k:k + 1, col:col + CW] * ext[half][start:start + TM]
            conv.append(u)
        pending.append((_silu(conv[0]) * conv[1]).astype(BF16))
        if len(pending) == FFN_DOWN_GROUP or c + 1 == n_chunks:
            lo = (c + 1 - len(pending)) * CW
            acc = acc + _dot(jnp.concatenate(pending, axis=1), wdown[lo:(c + 1) * CW, :])
            pending = []
        ext = ext_next
    o_ref[...] = h + acc


def _ffn(h2, norm_ffn, w_ffn_up, ffn_conv_w, ffn_conv_b, w_ffn_down, S):
    T = h2.shape[0]
    TM = ROW_TILE
    HALO = BF16_SUBLANES
    per_batch = S // TM
    return pl.pallas_call(
        functools.partial(_ffn_kernel, per_batch=per_batch),
        grid=(T // TM,),
        in_specs=[pl.BlockSpec((TM, D_MODEL), lambda i: (i, 0)),
                  pl.BlockSpec((HALO, D_MODEL), lambda i: (jnp.maximum(i * (TM // HALO) - 1, 0), 0)),
                  _const_spec((1, D_MODEL)), _const_spec((D_MODEL, 2 * FFN_HIDDEN)),
                  _const_spec((FFN_CONV, 2 * FFN_HIDDEN)), _const_spec((1, 2 * FFN_HIDDEN)),
                  _const_spec((FFN_HIDDEN, D_MODEL))],
        out_specs=pl.BlockSpec((TM, D_MODEL), lambda i: (i, 0)),
        out_shape=jax.ShapeDtypeStruct((T, D_MODEL), F32),
        compiler_params=_params(("parallel",)),
        name="ffn",
    )(h2, h2, norm_ffn.reshape(1, D_MODEL), w_ffn_up.astype(BF16), ffn_conv_w,
      ffn_conv_b.reshape(1, 2 * FFN_HIDDEN), w_ffn_down.astype(BF16))


def _layer(x, mem, norm_mix, w_in, ssd_conv_w, ssd_conv_b, ssd_dt_bias, ssd_a_log, ssd_d, ssd_norm, w_ssd_o,
           nsa_q_norm, nsa_k_norm, nsa_cmp_pe, nsa_cmp_w1, nsa_cmp_w2, w_nsa_o, mem_norm, w_mem_kv,
           mem_q_norm, mem_k_norm, w_mem_o, w_out, norm_ffn, w_ffn_up, ffn_conv_w, ffn_conv_b, w_ffn_down):
    B, S, D = x.shape
    x2 = x.reshape(B * S, D)
    z, xbc, small, q, kvc, kv, qm, gm = _in_proj(x2, norm_mix, w_in, nsa_q_norm, nsa_k_norm, mem_q_norm)
    ssd_n = _ssd(z.reshape(B, S, -1), xbc.reshape(B, S, -1), small.reshape(B, S, -1), ssd_conv_w, ssd_conv_b,
                 ssd_dt_bias, ssd_a_log, ssd_d, ssd_norm)
    kvcmp = _cmp(kvc.reshape(B, S, -1), nsa_cmp_pe, nsa_cmp_w1, nsa_cmp_w2, nsa_k_norm[0])
    nsa_o = _nsa(q, kvcmp, kv, small, B, S)
    k_mem, v_mem = _mem_kv(mem, mem_norm, w_mem_kv, mem_k_norm)
    h2 = _merge(x2, ssd_n.reshape(B * S, -1), nsa_o.reshape(B * S, -1), qm, gm, k_mem, v_mem,
                w_ssd_o, w_nsa_o, w_mem_o, w_out, S)
    out = _ffn(h2, norm_ffn, w_ffn_up, ffn_conv_w, ffn_conv_b, w_ffn_down, S)
    return out.reshape(B, S, D)


def kernel(x, mem, norm_mix, w_in, ssd_conv_w, ssd_conv_b, ssd_dt_bias, ssd_a_log, ssd_d, ssd_norm, w_ssd_o, nsa_q_norm, nsa_k_norm, nsa_cmp_pe, nsa_cmp_w1, nsa_cmp_w2, w_nsa_o, mem_norm, w_mem_kv, mem_q_norm, mem_k_norm, w_mem_o, w_out, norm_ffn, w_ffn_up, ffn_conv_w, ffn_conv_b, w_ffn_down):
    h = x
    for i in range(norm_mix.shape[0]):
        h = _layer(h, mem, norm_mix[i], w_in[i], ssd_conv_w[i], ssd_conv_b[i], ssd_dt_bias[i], ssd_a_log[i],
                   ssd_d[i], ssd_norm[i], w_ssd_o[i], nsa_q_norm[i], nsa_k_norm[i], nsa_cmp_pe[i],
                   nsa_cmp_w1[i], nsa_cmp_w2[i], w_nsa_o[i], mem_norm[i], w_mem_kv[i], mem_q_norm[i],
                   mem_k_norm[i], w_mem_o[i], w_out[i], norm_ffn[i], w_ffn_up[i], ffn_conv_w[i],
                   ffn_conv_b[i], w_ffn_down[i])
    return h
```

```python
import functools

import jax
import jax.numpy as jnp
import numpy as np
from jax import lax
from jax.experimental import pallas as pl
from jax.experimental.pallas import tpu as pltpu

F32 = jnp.float32
BF16 = jnp.bfloat16

D_MODEL = 1024
SSD_HEADS = 16
SSD_HEAD_DIM = 64
SSD_GROUPS = 2
SSD_STATE = 128
SSD_CONV = 4
SSD_CHUNK = 128
SSD_INNER = SSD_HEADS * SSD_HEAD_DIM
SSD_XBC = SSD_INNER + 2 * SSD_GROUPS * SSD_STATE
NSA_HEADS = 8
NSA_KV_GROUPS = 2
NSA_HG = NSA_HEADS // NSA_KV_GROUPS
NSA_HEAD_DIM = 64
NSA_CMP_LEN = 32
NSA_CMP_STRIDE = 16
NSA_CMP_HIDDEN = 256
NSA_SLC_BLOCK = 64
NSA_TOP_N = 16
NSA_WINDOW = 512
NSA_INNER = NSA_HEADS * NSA_HEAD_DIM
NSA_KV = 3 * 2 * NSA_KV_GROUPS * NSA_HEAD_DIM
NSA_GATES = 3 * NSA_HEADS
MEM_HEADS = 4
MEM_HEAD_DIM = 128
MEM_INNER = MEM_HEADS * MEM_HEAD_DIM
N_BRANCH = 3
FFN_HIDDEN = 2816
FFN_CONV = 3
NORM_EPS = 1e-6
NEG = -1e30
BIG = 1e9

IN_SIZES = (SSD_INNER, SSD_XBC, SSD_HEADS, NSA_INNER, NSA_KV, NSA_GATES, MEM_INNER, N_BRANCH * D_MODEL)

LANES = 128
BF16_SUBLANES = 16
VMEM_LIMIT = 56 * 1024 * 1024


def _bf16_terms(x, n):
    terms, rest = [], np.float32(x)
    for _ in range(n):
        t = np.float32(np.asarray(rest).astype(jnp.bfloat16))
        terms.append(float(t))
        rest = np.float32(rest - t)
    return tuple(terms)


LOG2E = float(np.float32(np.log2(np.e)))
LOG2E_TERMS = _bf16_terms(LOG2E, 3)

ROW_TILE = 512
NSA_TQ = 256
NSA_TK = 512
SSD_CHUNKS_PER_STEP = 4
FFN_CHUNK = 256
FFN_DOWN_GROUP = 6


def _dot(a, b):
    return jnp.dot(a, b, preferred_element_type=F32)


def _dot_nt(a, b):
    return lax.dot_general(a, b, (((1,), (1,)), ((), ())), preferred_element_type=F32)


def _split3(x):
    hi = x.astype(BF16)
    r1 = x - hi.astype(F32)
    mid = r1.astype(BF16)
    lo = (r1 - mid.astype(F32)).astype(BF16)
    return hi, mid, lo


def _dot3(x, w):
    hi, mid, lo = _split3(x)
    return _dot(hi, w) + _dot(mid, w) + _dot(lo, w)


def _silu(x):
    return x * jax.nn.sigmoid(x)


def _rms_rows(x, gain):
    return x * lax.rsqrt(jnp.mean(x * x, axis=-1, keepdims=True) + NORM_EPS) * gain


def _const_spec(shape):
    nd = len(shape)
    return pl.BlockSpec(shape, lambda *_: (0,) * nd, pipeline_mode=pl.Buffered(1))


def _params(sem, flags=None):
    return pltpu.CompilerParams(dimension_semantics=sem, vmem_limit_bytes=VMEM_LIMIT, flags=flags)


def _head_rms(v, gain, d):
    if d == LANES:
        ss = jnp.sum(v * v, axis=-1, keepdims=True)
        return v * lax.rsqrt(ss / d + NORM_EPS) * gain
    lane = lax.broadcasted_iota(jnp.int32, v.shape, 1)
    head = lane < d
    ss = jnp.sum(jnp.where(head, v * v, 0.0), axis=-1, keepdims=True)
    return jnp.where(head, v * lax.rsqrt(ss / d + NORM_EPS) * gain, v)


def _pair_rms(v, gain, d):
    lo = lax.broadcasted_iota(jnp.int32, v.shape, 1) < d
    sq = v * v
    ss_lo = jnp.sum(jnp.where(lo, sq, 0.0), axis=-1, keepdims=True)
    ss_hi = jnp.sum(jnp.where(lo, 0.0, sq), axis=-1, keepdims=True)
    return v * jnp.where(lo, lax.rsqrt(ss_lo / d + NORM_EPS), lax.rsqrt(ss_hi / d + NORM_EPS)) * gain


def _in_proj_kernel(x_ref, g_ref, wz, wxbc, wsm, wq, wkvc, wkv, wqm, wgm, qn_ref, kn_ref, mqn_ref,
                    z_o, xbc_o, sm_o, q_o, kvc_o, kv_o, qm_o, gm_o):
    xn = _rms_rows(x_ref[...], g_ref[...]).astype(BF16)
    z_o[...] = _dot(xn, wz[...]).astype(z_o.dtype)
    xbc_o[...] = _dot(xn, wxbc[...]).astype(xbc_o.dtype)
    sm_o[...] = _dot(xn, wsm[...])
    gm_o[...] = _dot(xn, wgm[...]).astype(gm_o.dtype)
    kvc_o[...] = _dot(xn, wkvc[...]).astype(kvc_o.dtype)
    q = _dot(xn, wq[...])
    for p in range(NSA_HEADS // 2):
        sl = slice(p * LANES, (p + 1) * LANES)
        q_o[:, sl] = _pair_rms(q[:, sl], qn_ref[...], NSA_HEAD_DIM).astype(q_o.dtype)
    kv = _dot(xn, wkv[...])
    for j in range(2 * NSA_KV_GROUPS):
        sl = slice(j * LANES, (j + 1) * LANES)
        kv_o[:, sl] = _head_rms(kv[:, sl], kn_ref[j % 2:j % 2 + 1, :], NSA_HEAD_DIM).astype(kv_o.dtype)
    qm = _dot(xn, wqm[...])
    for h in range(MEM_HEADS):
        sl = slice(h * LANES, (h + 1) * LANES)
        qm_o[:, sl] = _head_rms(qm[:, sl], mqn_ref[...], MEM_HEAD_DIM).astype(qm_o.dtype)


def _in_proj(x2, norm_mix, w_in, nsa_q_norm, nsa_k_norm, mem_q_norm):
    T = x2.shape[0]
    TM = ROW_TILE
    o = [0]
    for s in IN_SIZES:
        o.append(o[-1] + s)
    wz = w_in[:, o[0]:o[1]]
    wxbc = w_in[:, o[1]:o[2]]
    wq = w_in[:, o[3]:o[4]]
    wkv5 = w_in[:, o[4]:o[5]].reshape(D_MODEL, 3, 2, NSA_KV_GROUPS, NSA_HEAD_DIM)
    wkvc = wkv5[:, 0].transpose(0, 2, 1, 3).reshape(D_MODEL, NSA_KV_GROUPS * LANES)
    wkv = wkv5[:, 1:].transpose(0, 3, 1, 2, 4).reshape(D_MODEL, NSA_KV_GROUPS * 2 * LANES)
    wgn = w_in[:, o[5]:o[6]].reshape(D_MODEL, 3, NSA_KV_GROUPS, NSA_HG).transpose(0, 2, 1, 3).reshape(D_MODEL, NSA_GATES)
    wsm = jnp.pad(jnp.concatenate([w_in[:, o[2]:o[3]], wgn], axis=1), ((0, 0), (0, LANES - SSD_HEADS - NSA_GATES)))
    wqm = w_in[:, o[6]:o[7]]
    wgm = w_in[:, o[7]:o[8]]
    weights = [w.astype(BF16) for w in (wz, wxbc, wsm, wq, wkvc, wkv, wqm, wgm)]
    qn = jnp.tile(nsa_q_norm * (NSA_HEAD_DIM ** -0.5 * LOG2E), 2).reshape(1, LANES)
    kn = jnp.concatenate([nsa_k_norm[1:3], jnp.ones((2, LANES - NSA_HEAD_DIM), F32)], axis=1)
    mqn = (mem_q_norm * (MEM_HEAD_DIM ** -0.5 * LOG2E)).reshape(1, LANES)
    widths = [w.shape[1] for w in weights]
    dtypes = [BF16, BF16, F32, BF16, BF16, BF16, BF16, BF16]
    out_shape = [jax.ShapeDtypeStruct((T, w), dt) for w, dt in zip(widths, dtypes)]
    row = lambda w: pl.BlockSpec((TM, w), lambda i: (i, 0))
    return pl.pallas_call(
        _in_proj_kernel,
        grid=(T // TM,),
        in_specs=[row(D_MODEL), _const_spec((1, D_MODEL))] + [_const_spec(w.shape) for w in weights]
        + [_const_spec((1, LANES)), _const_spec((2, LANES)), _const_spec((1, LANES))],
        out_specs=[row(w) for w in widths],
        out_shape=out_shape,
        compiler_params=_params(("parallel",)),
        name="in_proj",
    )(x2, norm_mix.reshape(1, D_MODEL), *weights, qn, kn, mqn)


def _softplus(x):
    return jnp.maximum(x, 0.0) + jnp.log1p(jnp.exp(-jnp.abs(x)))


def _ssd_kernel(z_ref, xbc_ref, halo_ref, dt_ref, cw_ref, cb_ref, dtb_ref, alog_ref, dexp_ref, ng_ref, r_ref,
                o_ref, state):
    L = SSD_CHUNK
    HALO = BF16_SUBLANES
    c = pl.program_id(1)

    @pl.when(c == 0)
    def _():
        state[...] = jnp.zeros_like(state)

    for sc in range(z_ref.shape[0] // L):
        rows = slice(sc * L, (sc + 1) * L)
        if sc == 0:
            halo = jnp.where(c == 0, jnp.zeros_like(halo_ref[...]), halo_ref[...])
        else:
            halo = xbc_ref[sc * L - HALO:sc * L, :]
        o_ref[rows, :] = _ssd_chunk(z_ref[rows, :], xbc_ref[rows, :], halo, dt_ref[rows, :], cw_ref, cb_ref, dtb_ref,
                                    alog_ref, dexp_ref, ng_ref, r_ref, state).astype(o_ref.dtype)


def _ssd_chunk(z, xbc, halo, dt_raw, cw_ref, cb_ref, dtb_ref, alog_ref, dexp_ref, ng_ref, r_ref, state):
    L = SSD_CHUNK
    HALO = BF16_SUBLANES
    xe = jnp.concatenate([halo, xbc], axis=0)
    src = lax.broadcasted_iota(jnp.int32, (L, HALO + L), 1) - lax.broadcasted_iota(jnp.int32, (L, HALO + L), 0)
    acc = cb_ref[...] + cw_ref[SSD_CONV - 1:SSD_CONV, :] * xbc.astype(F32)
    for k in range(SSD_CONV - 1):
        shift = jnp.where(src == HALO - (SSD_CONV - 1) + k, 1.0, 0.0).astype(BF16)
        acc = acc + cw_ref[k:k + 1, :] * _dot(shift, xe)
    xa = _silu(acc)
    xs = xa[:, :SSD_INNER]

    lane = lax.broadcasted_iota(jnp.int32, (L, LANES), 1)
    rowi = lax.broadcasted_iota(jnp.int32, (L, L), 0)
    coli = lax.broadcasted_iota(jnp.int32, (L, L), 1)
    tril = rowi >= coli
    tril_w = jnp.where(tril, 1.0, 0.0).astype(BF16)

    head_lane = lane < SSD_HEADS
    dt = jnp.where(head_lane, _softplus(dt_raw + dtb_ref[...]), 0.0)
    d_a = dt * jnp.where(head_lane[0:1], -jnp.exp(alog_ref[...]) * LOG2E, 0.0)
    cs = sum(_dot(tril_w, part) for part in _split3(d_a))
    cs_t = cs.T
    r = r_ref[...]
    dt_x = _dot3(dt, r)
    cs_x = _dot3(cs, r)
    ecs_x = jnp.exp2(cs_x)
    decay_x = jnp.exp2(cs_x[L - 1:L, :] - cs_x)

    xdt = xs * dt_x
    xdtd = (xdt * decay_x).astype(BF16)
    xdt16 = xdt.astype(BF16)
    y_skip = xs * dexp_ref[...]
    lane_lo = lane < SSD_HEAD_DIM

    hpg = SSD_HEADS // SSD_GROUPS
    gw = hpg * SSD_HEAD_DIM
    y_blocks = []
    for g in range(SSD_GROUPS):
        b_g = xa[:, SSD_INNER + g * SSD_STATE:SSD_INNER + (g + 1) * SSD_STATE]
        c_g = xa[:, SSD_INNER + (SSD_GROUPS + g) * SSD_STATE:SSD_INNER + (SSD_GROUPS + g + 1) * SSD_STATE]
        b16 = b_g.astype(BF16)
        c16 = c_g.astype(BF16)
        cb = _dot_nt(c16, b16)
        st = state[:, g * gw:(g + 1) * gw]
        y_off = _dot(c16, st.astype(BF16)) * ecs_x[:, g * gw:(g + 1) * gw]
        bt16 = b_g.T.astype(BF16)
        state[:, g * gw:(g + 1) * gw] = (st * ecs_x[L - 1:L, g * gw:(g + 1) * gw]
                                         + _dot(bt16, xdtd[:, g * gw:(g + 1) * gw]))
        for hp in range(hpg // 2):
            col = g * gw + hp * LANES
            xp = xdt16[:, col:col + LANES]
            pair = []
            for j in range(2):
                h = g * hpg + hp * 2 + j
                seg = cs[:, h:h + 1] - cs_t[h:h + 1, :]
                lm = jnp.where(tril, jnp.exp2(jnp.where(tril, seg, 0.0)), 0.0)
                pair.append(_dot((cb * lm).astype(BF16), xp))
            y_diag = jnp.where(lane_lo, pair[0], pair[1])
            y_blocks.append(y_diag + y_off[:, hp * LANES:(hp + 1) * LANES] + y_skip[:, col:col + LANES])
    y = jnp.concatenate(y_blocks, axis=1)
    yz = y * _silu(z.astype(F32))
    return _rms_rows(yz, ng_ref[...])


def _ssd(z, xbc, dt, conv_w, conv_b, dt_bias, a_log, d_skip, norm_g):
    B, S, _ = z.shape
    L = SSD_CHUNK
    HALO = BF16_SUBLANES
    pad = lambda v: jnp.pad(v, (0, LANES - SSD_HEADS)).reshape(1, LANES)
    d_exp = jnp.repeat(d_skip, SSD_HEAD_DIM).reshape(1, SSD_INNER)
    expand = (jnp.arange(LANES)[:, None] == (jnp.arange(SSD_INNER)[None, :] // SSD_HEAD_DIM)).astype(BF16)
    step = SSD_CHUNKS_PER_STEP * SSD_CHUNK
    blk = lambda w: pl.BlockSpec((None, step, w), lambda b, c: (b, c, 0))
    halo_spec = pl.BlockSpec((None, HALO, SSD_XBC), lambda b, c: (b, jnp.maximum(c * (step // HALO) - 1, 0), 0))
    return pl.pallas_call(
        _ssd_kernel,
        grid=(B, S // step),
        in_specs=[blk(SSD_INNER), blk(SSD_XBC), halo_spec, blk(LANES),
                  _const_spec((SSD_CONV, SSD_XBC)), _const_spec((1, SSD_XBC)), _const_spec((1, LANES)),
                  _const_spec((1, LANES)), _const_spec((1, SSD_INNER)), _const_spec((1, SSD_INNER)),
                  _const_spec((LANES, SSD_INNER))],
        out_specs=blk(SSD_INNER),
        out_shape=jax.ShapeDtypeStruct((B, S, SSD_INNER), BF16),
        scratch_shapes=[pltpu.VMEM((SSD_STATE, SSD_INNER), F32)],
        compiler_params=_params(("parallel", "arbitrary")),
        name="ssd",
    )(z, xbc, xbc, dt, conv_w, conv_b.reshape(1, SSD_XBC), pad(dt_bias), pad(a_log), d_exp,
      norm_g.reshape(1, SSD_INNER), expand)


def _cmp_kernel(kvc_ref, pea_ref, peb_ref, w1a_ref, w1b_ref, w2_ref, kn_ref, o_ref):
    n_chunk = kvc_ref.shape[0]
    per = NSA_CMP_STRIDE
    row = lax.broadcasted_iota(jnp.int32, (n_chunk, LANES), 0)
    for g in range(NSA_KV_GROUPS):
        x = jnp.concatenate([kvc_ref[:, (NSA_KV_GROUPS * l + g) * LANES:(NSA_KV_GROUPS * l + g + 1) * LANES]
                             for l in range(per)], axis=1).astype(F32)
        a = _dot((x + pea_ref[...]).astype(BF16), w1a_ref[...])
        b = _dot((x + peb_ref[...]).astype(BF16), w1b_ref[...])
        hid = _silu(a + pltpu.roll(b, n_chunk - 1, axis=0))
        cmp = _dot(hid.astype(BF16), w2_ref[...])
        cmp = _head_rms(cmp, kn_ref[...], NSA_HEAD_DIM)
        o_ref[g] = jnp.where(row < n_chunk - 1, cmp, 0.0).astype(o_ref.dtype)


def _cmp(kvc, nsa_cmp_pe, nsa_cmp_w1, nsa_cmp_w2, k_norm0):
    B, S, _ = kvc.shape
    n_chunk = S // NSA_CMP_STRIDE
    per = NSA_CMP_STRIDE
    dk = NSA_HEAD_DIM
    kvc_r = kvc.reshape(B, n_chunk, per * NSA_KV_GROUPS * LANES)
    pe = jnp.concatenate([nsa_cmp_pe[0], nsa_cmp_pe[1]], axis=1)
    pea = pe[:per].reshape(1, per * LANES)
    peb = pe[per:].reshape(1, per * LANES)
    w1 = nsa_cmp_w1.reshape(2, NSA_CMP_LEN, dk, NSA_CMP_HIDDEN)
    zero = jnp.zeros((NSA_CMP_LEN, dk, NSA_CMP_HIDDEN), F32)
    w1bd = jnp.concatenate([jnp.concatenate([w1[0], zero], axis=2), jnp.concatenate([zero, w1[1]], axis=2)], axis=1)
    w1a = w1bd[:per].reshape(per * LANES, 2 * NSA_CMP_HIDDEN).astype(BF16)
    w1b = w1bd[per:].reshape(per * LANES, 2 * NSA_CMP_HIDDEN).astype(BF16)
    z2 = jnp.zeros((NSA_CMP_HIDDEN, dk), F32)
    w2bd = jnp.concatenate([jnp.concatenate([nsa_cmp_w2[0], z2], axis=1),
                            jnp.concatenate([z2, nsa_cmp_w2[1]], axis=1)], axis=0).astype(BF16)
    kn = jnp.concatenate([k_norm0, jnp.ones((LANES - dk,), F32)]).reshape(1, LANES)
    return pl.pallas_call(
        _cmp_kernel,
        grid=(B,),
        in_specs=[pl.BlockSpec((None, n_chunk, per * NSA_KV_GROUPS * LANES), lambda b: (b, 0, 0)),
                  _const_spec(pea.shape), _const_spec(peb.shape), _const_spec(w1a.shape), _const_spec(w1b.shape),
                  _const_spec(w2bd.shape), _const_spec(kn.shape)],
        out_specs=pl.BlockSpec((None, NSA_KV_GROUPS, n_chunk, LANES), lambda b: (b, 0, 0, 0)),
        out_shape=jax.ShapeDtypeStruct((B, NSA_KV_GROUPS, n_chunk, LANES), BF16),
        compiler_params=_params(("parallel",)),
        name="nsa_cmp",
    )(kvc_r, pea, peb, w1a, w1b, w2bd, kn)


def _top_blocks(imp, jt):
    n_slc, TQ = imp.shape
    blk_q = lax.broadcasted_iota(jnp.int32, (n_slc, TQ), 0)
    imp = jnp.where(blk_q == 0, BIG, imp)
    imp = jnp.where(blk_q == jt, BIG, imp)
    imp = jnp.where(blk_q == jt - 1, BIG, imp)
    imp = jnp.where(blk_q > jt, NEG, imp)
    SUB = 8
    row8 = lax.broadcasted_iota(jnp.int32, (SUB, 1), 0)
    groups = [imp[k * SUB:(k + 1) * SUB] for k in range(n_slc // SUB)]
    ranks = [jnp.zeros((SUB, TQ), F32) for _ in groups]
    for i in range(n_slc):
        vi = jnp.broadcast_to(imp[i:i + 1, :], (SUB, TQ))
        for k, grp in enumerate(groups):
            if k * SUB > i:
                ahead = jnp.where(vi >= grp, 1.0, 0.0)
            elif k * SUB + SUB - 1 < i:
                ahead = jnp.where(vi > grp, 1.0, 0.0)
            else:
                ahead = jnp.where(row8 + k * SUB > i, jnp.where(vi >= grp, 1.0, 0.0), jnp.where(vi > grp, 1.0, 0.0))
            ranks[k] = ranks[k] + ahead
    rank = jnp.concatenate(ranks, axis=0)
    return jnp.where(rank < NSA_TOP_N, jnp.where(blk_q <= jt, 1.0, 0.0), 0.0)


def _nsa_kernel(q_ref, kvc_ref, kv_ref, gate_ref, ov_ref, o_ref, k2c, vct, k2s, vst, k2w, vwt, s_a, s_b, m_st, acc_st,
                *, seq):
    TQ, TK, W = NSA_TQ, NSA_TK, NSA_WINDOW
    G, HG, DK = NSA_KV_GROUPS, NSA_HG, NSA_HEAD_DIM
    PAIR = 2
    n_cmp_pad = kvc_ref.shape[1]
    n_slc = seq // NSA_SLC_BLOCK
    t0 = pl.program_id(1) * TQ
    slopes = [[2.0 ** -(g * HG + h + 1) for h in range(HG)] for g in range(G)]

    @pl.when(pl.program_id(1) == 0)
    def _build_keys():
        lane = lax.broadcasted_iota(jnp.int32, (LANES, LANES), 1)
        row = lax.broadcasted_iota(jnp.int32, (LANES, LANES), 0)

        def with_pos(kv, pos):
            rel = lane - DK
            feat = jnp.where(rel % 2 == 0, pos // NSA_SLC_BLOCK, pos % NSA_SLC_BLOCK)
            feat = jnp.where(rel < 2 * len(LOG2E_TERMS), feat, 0).astype(F32)
            return jnp.where(rel < 0, kv, feat).astype(BF16)

        ones_row = jnp.where(lax.broadcasted_iota(jnp.int32, (BF16_SUBLANES, LANES), 0) == 0, 1.0, 0.0)

        def values_t(kv):
            return jnp.concatenate([kv.T[DK:], ones_row], axis=0).astype(BF16)

        for g in range(G):
            for c in range(n_cmp_pad // LANES):
                rows = slice(c * LANES, (c + 1) * LANES)
                kv = kvc_ref[g, rows, :].astype(F32)
                k2c[g, rows, :] = with_pos(kv, (row + c * LANES) * NSA_CMP_STRIDE + (NSA_CMP_LEN - 1))
                vct[g, :, rows] = values_t(kv)

        def chunk(c, carry):
            r0 = pl.multiple_of(c * LANES, LANES)
            pos = row + r0
            one_hot = jnp.where(lane == pos // NSA_SLC_BLOCK, 1.0, 0.0).astype(BF16)
            for g in range(G):
                kv = kv_ref[pl.ds(r0, LANES), 2 * g * LANES:(2 * g + 1) * LANES].astype(F32)
                k2s[g, pl.ds(r0, LANES), 0:LANES] = with_pos(kv, pos)
                k2s[g, pl.ds(r0, LANES), LANES:2 * LANES] = one_hot
                vst[g, :, pl.ds(r0, LANES)] = values_t(kv)
                kvw = kv_ref[pl.ds(r0, LANES), (2 * g + 1) * LANES:(2 * g + 2) * LANES].astype(F32)
                k2w[g, pl.ds(r0, LANES), :] = with_pos(kvw, pos)
                vwt[g, :, pl.ds(r0, LANES)] = values_t(kvw)
            return carry

        lax.fori_loop(0, seq // LANES, chunk, 0)

    rel_q = lax.broadcasted_iota(jnp.int32, (1, LANES), 1) - DK
    lane_lo = lax.broadcasted_iota(jnp.int32, (TQ, LANES), 1) < DK
    q1 = [[None] * HG for _ in range(G)]
    for g in range(G):
        for h in range(HG):
            feat = jnp.zeros((1, LANES), F32)
            for i, term in enumerate(LOG2E_TERMS):
                feat = jnp.where(rel_q == 2 * i, NSA_SLC_BLOCK * slopes[g][h] * term, feat)
                feat = jnp.where(rel_q == 2 * i + 1, slopes[g][h] * term, feat)
            head = g * HG + h
            qpair = q_ref[:, (head // 2) * LANES:(head // 2 + 1) * LANES].astype(F32)
            if head % 2:
                qpair = pltpu.roll(qpair, DK, axis=1)
            q1[g][h] = (jnp.where(lane_lo, qpair, 0.0) + feat).astype(BF16)
    chains = [(g, hp) for g in range(G) for hp in range(HG // PAIR)]
    q1c = {(g, hp): jnp.concatenate(q1[g][hp * PAIR:(hp + 1) * PAIR], axis=0) for g, hp in chains}
    tq = t0 + lax.broadcasted_iota(jnp.int32, (1, TQ), 1)
    jt = tq // NSA_SLC_BLOCK

    def tile_pair(mask):
        return jnp.concatenate([mask] * PAIR, axis=1)

    cend = lax.broadcasted_iota(jnp.int32, (n_cmp_pad, 1), 0) * NSA_CMP_STRIDE + (NSA_CMP_LEN - 1)
    valid = (tq - cend) >= 0
    o_cmp = {}
    psum = [jnp.zeros((n_cmp_pad, TQ), F32) for _ in range(G)]
    sc_all = [_dot_nt(k2c[g], q1c[g, hp]) for g, hp in chains]
    band = W + TQ
    w0 = pl.multiple_of(jnp.maximum(t0 - W, 0), TQ)
    dw = tq - (w0 + lax.broadcasted_iota(jnp.int32, (band, 1), 0))
    in_win = tile_pair(jnp.where(dw >= 0, jnp.where(dw < W, 0.0, NEG), NEG))
    sw_all = [_dot_nt(k2w[g, pl.ds(w0, band), :], q1c[g, hp]) for g, hp in chains]
    for (g, hp), sct in zip(chains, sc_all):
        p_list = []
        for j in range(PAIR):
            s = jnp.where(valid, sct[:, j * TQ:(j + 1) * TQ], NEG)
            m = jnp.max(s, axis=0, keepdims=True)
            p_list.append(jnp.where(valid, jnp.exp2(s - m), 0.0))
        pv = _dot(vct[g], jnp.concatenate(p_list, axis=1).astype(BF16))
        l = pv[DK:DK + 1]
        inv = 1.0 / jnp.where(l > 0.0, l, 1.0)
        o_cmp[g, hp] = pv[0:DK] * inv
        for j in range(PAIR):
            psum[g] = psum[g] + p_list[j] * inv[:, j * TQ:(j + 1) * TQ]
    q2c = {}
    for g in range(G):
        imp = sum(_dot(ov_ref[...], part) for part in _split3(psum[g]))
        sel = _top_blocks(imp[0:n_slc], jt)
        if n_slc < LANES:
            sel = jnp.concatenate([sel, jnp.zeros((LANES - n_slc, TQ), F32)], axis=0)
        sel_neg = jnp.where(sel.T > 0.5, 0.0, -(2.0 ** 100)).astype(BF16)
        for hp in range(HG // PAIR):
            q2c[g, hp] = jnp.concatenate([jnp.concatenate([q1[g][hp * PAIR + j], sel_neg], axis=1)
                                          for j in range(PAIR)], axis=0)

    def slc_scores(kt, buf):
        k0 = pl.multiple_of(kt * TK, TK)
        for ci, (g, hp) in enumerate(chains):
            buf[ci] = _dot_nt(k2s[g, pl.ds(k0, TK), :], q2c[g, hp])

    def slc_softmax(kt, buf, masked):
        k0 = pl.multiple_of(kt * TK, TK)
        if masked:
            kpos = k0 + lax.broadcasted_iota(jnp.int32, (TK, 1), 0)
            causal_add = tile_pair(jnp.where(kpos <= tq, 0.0, NEG))
        for ci, (g, hp) in enumerate(chains):
            s = buf[ci]
            if masked:
                s = s + causal_add
            m = m_st[ci]
            m_new = jnp.maximum(m, jnp.max(s, axis=0, keepdims=True))
            p = jnp.exp2(s - m_new)
            acc_st[ci] = jnp.exp2(m - m_new) * acc_st[ci] + _dot(vst[g, :, pl.ds(k0, TK)], p.astype(BF16))
            m_st[ci] = m_new

    n_past = t0 // TK
    m_st[...] = jnp.full(m_st.shape, NEG, F32)
    acc_st[...] = jnp.zeros(acc_st.shape, F32)
    slc_scores(0, s_a)

    def slc_pair(j, carry):
        slc_scores(2 * j + 1, s_b)
        slc_softmax(2 * j, s_a, False)
        slc_scores(2 * j + 2, s_a)
        slc_softmax(2 * j + 1, s_b, False)
        return carry

    o_win = {}
    for (g, hp), sw in zip(chains, sw_all):
        sw = sw + in_win
        pw = jnp.exp2(sw - jnp.max(sw, axis=0, keepdims=True))
        pv = _dot(vwt[g, :, pl.ds(w0, band)], pw.astype(BF16))
        o_win[g, hp] = pv[0:DK] / pv[DK:DK + 1]

    lax.fori_loop(0, n_past // 2, slc_pair, 0)

    @pl.when(n_past % 2 == 1)
    def _odd_tail():
        slc_scores(n_past, s_b)
        slc_softmax(n_past - 1, s_a, False)
        slc_softmax(n_past, s_b, True)

    @pl.when(n_past % 2 == 0)
    def _even_tail():
        slc_softmax(n_past, s_a, True)

    o_slc = {ch: acc_st[ci, 0:DK] / acc_st[ci, DK:DK + 1] for ci, ch in enumerate(chains)}

    comb = []
    gt = jax.nn.sigmoid(gate_ref[...]).T
    for g in range(G):
        for h in range(HG):
            ch = (g, h // PAIR)
            cols = slice((h % PAIR) * TQ, (h % PAIR + 1) * TQ)
            gate = [gt[r:r + 1] for r in (SSD_HEADS + (g * 3 + br) * HG + h for br in range(3))]
            comb.append(gate[0] * o_cmp[ch][:, cols] + gate[1] * o_slc[ch][:, cols]
                        + gate[2] * o_win[ch][:, cols])
    o_ref[...] = jnp.concatenate(comb, axis=0).T.astype(o_ref.dtype)


def _nsa(q, kvcmp, kv, gates, B, S):
    TQ = NSA_TQ
    G = NSA_KV_GROUPS
    assert S % (NSA_CMP_STRIDE * LANES) == 0 and S % NSA_TK == 0 and NSA_TK % TQ == 0 and S >= NSA_WINDOW + TQ
    n_cmp_pad = S // NSA_CMP_STRIDE
    n_slc = S // NSA_SLC_BLOCK
    cs = jnp.arange(n_cmp_pad) * NSA_CMP_STRIDE
    ce = cs + NSA_CMP_LEN - 1
    ss = jnp.arange(LANES) * NSA_SLC_BLOCK
    overlap = jnp.clip(jnp.minimum(ce[None, :], ss[:, None] + NSA_SLC_BLOCK - 1)
                       - jnp.maximum(cs[None, :], ss[:, None]) + 1, 0).astype(F32) / NSA_CMP_LEN
    valid_pair = (jnp.arange(LANES)[:, None] < n_slc) & (jnp.arange(n_cmp_pad)[None, :] < n_cmp_pad - 1)
    ov_t = jnp.where(valid_pair, overlap, 0.0).astype(BF16)
    q3 = q.reshape(B, S, NSA_INNER)
    kv3 = kv.reshape(B, S, G * 2 * LANES)
    g3 = gates.reshape(B, S, LANES)
    vt_rows = NSA_HEAD_DIM + BF16_SUBLANES
    n_chains = NSA_HEADS // 2
    return pl.pallas_call(
        functools.partial(_nsa_kernel, seq=S),
        grid=(B, S // TQ),
        in_specs=[pl.BlockSpec((None, TQ, NSA_INNER), lambda b, i: (b, i, 0)),
                  pl.BlockSpec((None, G, n_cmp_pad, LANES), lambda b, i: (b, 0, 0, 0)),
                  pl.BlockSpec((None, S, G * 2 * LANES), lambda b, i: (b, 0, 0)),
                  pl.BlockSpec((None, TQ, LANES), lambda b, i: (b, i, 0)),
                  _const_spec(ov_t.shape)],
        out_specs=pl.BlockSpec((None, TQ, NSA_INNER), lambda b, i: (b, i, 0)),
        out_shape=jax.ShapeDtypeStruct((B, S, NSA_INNER), BF16),
        scratch_shapes=[pltpu.VMEM((G, n_cmp_pad, LANES), BF16), pltpu.VMEM((G, vt_rows, n_cmp_pad), BF16),
                        pltpu.VMEM((G, S, 2 * LANES), BF16), pltpu.VMEM((G, vt_rows, S), BF16),
                        pltpu.VMEM((G, S, LANES), BF16), pltpu.VMEM((G, vt_rows, S), BF16),
                        pltpu.VMEM((n_chains, NSA_TK, 2 * TQ), F32), pltpu.VMEM((n_chains, NSA_TK, 2 * TQ), F32),
                        pltpu.VMEM((n_chains, 1, 2 * TQ), F32), pltpu.VMEM((n_chains, vt_rows, 2 * TQ), F32)],
        compiler_params=_params(("parallel", "arbitrary")),
        name="nsa_attn",
    )(q3, kvcmp, kv3, g3, ov_t)


def _mem_kv_kernel(mem_ref, g_ref, w_ref, kn_ref, k_o, v_o):
    mn = _rms_rows(mem_ref[...], g_ref[...]).astype(BF16)
    kv = _dot(mn, w_ref[...])
    for h in range(MEM_HEADS):
        sl = slice(h * LANES, (h + 1) * LANES)
        k_o[:, sl] = _head_rms(kv[:, sl], kn_ref[...], MEM_HEAD_DIM).astype(k_o.dtype)
    v_o[...] = kv[:, MEM_INNER:].astype(v_o.dtype)


def _mem_kv(mem, mem_norm, w_mem_kv, mem_k_norm):
    B, M, _ = mem.shape
    spec_o = pl.BlockSpec((None, M, MEM_INNER), lambda b: (b, 0, 0))
    return pl.pallas_call(
        _mem_kv_kernel,
        grid=(B,),
        in_specs=[pl.BlockSpec((None, M, D_MODEL), lambda b: (b, 0, 0)), _const_spec((1, D_MODEL)),
                  _const_spec((D_MODEL, 2 * MEM_INNER)), _const_spec((1, LANES))],
        out_specs=[spec_o, spec_o],
        out_shape=[jax.ShapeDtypeStruct((B, M, MEM_INNER), BF16)] * 2,
        compiler_params=_params(("parallel",)),
        name="mem_kv",
    )(mem, mem_norm.reshape(1, D_MODEL), w_mem_kv.astype(BF16), mem_k_norm.reshape(1, LANES))


def _merge_kernel(x_ref, ssd_ref, nsa_ref, qm_ref, gm_ref, km_ref, vm_ref, wso, wno, wmo, wout, h_o):
    heads = [slice(h * LANES, (h + 1) * LANES) for h in range(MEM_HEADS)]
    scores = [_dot_nt(qm_ref[:, sl], km_ref[:, sl]) for sl in heads]
    y_ssd = _dot(ssd_ref[...], wso[...])
    y_nsa = _dot(nsa_ref[...], wno[...])
    o_mem = []
    for sl, s in zip(heads, scores):
        p = jnp.exp2(s - jnp.max(s, axis=-1, keepdims=True))
        o = _dot(p.astype(BF16), vm_ref[:, sl]) / jnp.sum(p, axis=-1, keepdims=True)
        o_mem.append(o.astype(BF16))
    y_mem = _dot(jnp.concatenate(o_mem, axis=1), wmo[...])
    gate = lambda j: jax.nn.sigmoid(gm_ref[:, j * D_MODEL:(j + 1) * D_MODEL].astype(F32))
    mixed = gate(0) * y_ssd + gate(1) * y_nsa + gate(2) * y_mem
    h_o[...] = x_ref[...] + _dot(mixed.astype(BF16), wout[...])


def _merge(x2, ssd_n, nsa_o, q_mem, g_merge, k_mem, v_mem, w_ssd_o, w_nsa_o, w_mem_o, w_out, S):
    T = x2.shape[0]
    TM = ROW_TILE
    M = k_mem.shape[1]
    per_batch = S // TM
    row = lambda w: pl.BlockSpec((TM, w), lambda i: (i, 0))
    mem_spec = pl.BlockSpec((None, M, MEM_INNER), lambda i: (i // per_batch, 0, 0))
    weights = [w.astype(BF16) for w in (w_ssd_o, w_nsa_o, w_mem_o, w_out)]
    return pl.pallas_call(
        _merge_kernel,
        grid=(T // TM,),
        in_specs=[row(D_MODEL), row(SSD_INNER), row(NSA_INNER), row(MEM_INNER), row(N_BRANCH * D_MODEL),
                  mem_spec, mem_spec] + [_const_spec(w.shape) for w in weights],
        out_specs=row(D_MODEL),
        out_shape=jax.ShapeDtypeStruct((T, D_MODEL), F32),
        compiler_params=_params(("parallel",)),
        name="merge",
    )(x2, ssd_n, nsa_o, q_mem, g_merge, k_mem, v_mem, *weights)


def _ffn_kernel(h_ref, halo_ref, g_ref, wup, cw_ref, cb_ref, wdown, o_ref, *, per_batch):
    TM = h_ref.shape[0]
    HALO = halo_ref.shape[0]
    CW = FFN_CHUNK
    first = pl.program_id(0) % per_batch == 0
    h = h_ref[...]
    halo_n = _rms_rows(halo_ref[...], g_ref[...]) * jnp.where(first, 0.0, 1.0)
    hn = jnp.concatenate([halo_n, _rms_rows(h, g_ref[...])], axis=0).astype(BF16)
    n_chunks = FFN_HIDDEN // CW

    def up_proj(c):
        return [_dot(hn, wup[:, half * FFN_HIDDEN + c * CW:half * FFN_HIDDEN + (c + 1) * CW]) for half in range(2)]

    acc = jnp.zeros((TM, D_MODEL), F32)
    ext = up_proj(0)
    pending = []
    for c in range(n_chunks):
        ext_next = up_proj(c + 1) if c + 1 < n_chunks else None
        conv = []
        for half in range(2):
            col = half * FFN_HIDDEN + c * CW
            u = jnp.broadcast_to(cb_ref[:, col:col + CW], (TM, CW))
            for k in range(FFN_CONV):
                start = HALO - (FFN_CONV - 1) + k
                u = u + cw_ref[k:k + 1, col:col + CW] * ext[half][start:start + TM]
            conv.append(u)
        pending.append((_silu(conv[0]) * conv[1]).astype(BF16))
        if len(pending) == FFN_DOWN_GROUP or c + 1 == n_chunks:
            lo = (c + 1 - len(pending)) * CW
            acc = acc + _dot(jnp.concatenate(pending, axis=1), wdown[lo:(c + 1) * CW, :])
            pending = []
        ext = ext_next
    o_ref[...] = h + acc


def _ffn(h2, norm_ffn, w_ffn_up, ffn_conv_w, ffn_conv_b, w_ffn_down, S):
    T = h2.shape[0]
    TM = ROW_TILE
    HALO = BF16_SUBLANES
    per_batch = S // TM
    return pl.pallas_call(
        functools.partial(_ffn_kernel, per_batch=per_batch),
        grid=(T // TM,),
        in_specs=[pl.BlockSpec((TM, D_MODEL), lambda i: (i, 0)),
                  pl.BlockSpec((HALO, D_MODEL), lambda i: (jnp.maximum(i * (TM // HALO) - 1, 0), 0)),
                  _const_spec((1, D_MODEL)), _const_spec((D_MODEL, 2 * FFN_HIDDEN)),
                  _const_spec((FFN_CONV, 2 * FFN_HIDDEN)), _const_spec((1, 2 * FFN_HIDDEN)),
                  _const_spec((FFN_HIDDEN, D_MODEL))],
        out_specs=pl.BlockSpec((TM, D_MODEL), lambda i: (i, 0)),
        out_shape=jax.ShapeDtypeStruct((T, D_MODEL), F32),
        compiler_params=_params(("parallel",)),
        name="ffn",
    )(h2, h2, norm_ffn.reshape(1, D_MODEL), w_ffn_up.astype(BF16), ffn_conv_w,
      ffn_conv_b.reshape(1, 2 * FFN_HIDDEN), w_ffn_down.astype(BF16))


def _layer(x, mem, norm_mix, w_in, ssd_conv_w, ssd_conv_b, ssd_dt_bias, ssd_a_log, ssd_d, ssd_norm, w_ssd_o,
           nsa_q_norm, nsa_k_norm, nsa_cmp_pe, nsa_cmp_w1, nsa_cmp_w2, w_nsa_o, mem_norm, w_mem_kv,
           mem_q_norm, mem_k_norm, w_mem_o, w_out, norm_ffn, w_ffn_up, ffn_conv_w, ffn_conv_b, w_ffn_down):
    B, S, D = x.shape
    x2 = x.reshape(B * S, D)
    z, xbc, small, q, kvc, kv, qm, gm = _in_proj(x2, norm_mix, w_in, nsa_q_norm, nsa_k_norm, mem_q_norm)
    ssd_n = _ssd(z.reshape(B, S, -1), xbc.reshape(B, S, -1), small.reshape(B, S, -1), ssd_conv_w, ssd_conv_b,
                 ssd_dt_bias, ssd_a_log, ssd_d, ssd_norm)
    kvcmp = _cmp(kvc.reshape(B, S, -1), nsa_cmp_pe, nsa_cmp_w1, nsa_cmp_w2, nsa_k_norm[0])
    nsa_o = _nsa(q, kvcmp, kv, small, B, S)
    k_mem, v_mem = _mem_kv(mem, mem_norm, w_mem_kv, mem_k_norm)
    h2 = _merge(x2, ssd_n.reshape(B * S, -1), nsa_o.reshape(B * S, -1), qm, gm, k_mem, v_mem,
                w_ssd_o, w_nsa_o, w_mem_o, w_out, S)
    out = _ffn(h2, norm_ffn, w_ffn_up, ffn_conv_w, ffn_conv_b, w_ffn_down, S)
    return out.reshape(B, S, D)


def kernel(x, mem, norm_mix, w_in, ssd_conv_w, ssd_conv_b, ssd_dt_bias, ssd_a_log, ssd_d, ssd_norm, w_ssd_o, nsa_q_norm, nsa_k_norm, nsa_cmp_pe, nsa_cmp_w1, nsa_cmp_w2, w_nsa_o, mem_norm, w_mem_kv, mem_q_norm, mem_k_norm, w_mem_o, w_out, norm_ffn, w_ffn_up, ffn_conv_w, ffn_conv_b, w_ffn_down):
    h = x
    for i in range(norm_mix.shape[0]):
        h = _layer(h, mem, norm_mix[i], w_in[i], ssd_conv_w[i], ssd_conv_b[i], ssd_dt_bias[i], ssd_a_log[i],
                   ssd_d[i], ssd_norm[i], w_ssd_o[i], nsa_q_norm[i], nsa_k_norm[i], nsa_cmp_pe[i],
                   nsa_cmp_w1[i], nsa_cmp_w2[i], w_nsa_o[i], mem_norm[i], w_mem_kv[i], mem_q_norm[i],
                   mem_k_norm[i], w_mem_o[i], w_out[i], norm_ffn[i], w_ffn_up[i], ffn_conv_w[i],
                   ffn_conv_b[i], w_ffn_down[i])
    return h
```

```python
import functools

import jax
import jax.numpy as jnp
import numpy as np
from jax import lax
from jax.experimental import pallas as pl
from jax.experimental.pallas import tpu as pltpu

F32 = jnp.float32
BF16 = jnp.bfloat16

D_MODEL = 1024
SSD_HEADS = 16
SSD_HEAD_DIM = 64
SSD_GROUPS = 2
SSD_STATE = 128
SSD_CONV = 4
SSD_CHUNK = 128
SSD_INNER = SSD_HEADS * SSD_HEAD_DIM
SSD_XBC = SSD_INNER + 2 * SSD_GROUPS * SSD_STATE
NSA_HEADS = 8
NSA_KV_GROUPS = 2
NSA_HG = NSA_HEADS // NSA_KV_GROUPS
NSA_HEAD_DIM = 64
NSA_CMP_LEN = 32
NSA_CMP_STRIDE = 16
NSA_CMP_HIDDEN = 256
NSA_SLC_BLOCK = 64
NSA_TOP_N = 16
NSA_WINDOW = 512
NSA_INNER = NSA_HEADS * NSA_HEAD_DIM
NSA_KV = 3 * 2 * NSA_KV_GROUPS * NSA_HEAD_DIM
NSA_GATES = 3 * NSA_HEADS
MEM_HEADS = 4
MEM_HEAD_DIM = 128
MEM_INNER = MEM_HEADS * MEM_HEAD_DIM
N_BRANCH = 3
FFN_HIDDEN = 2816
FFN_CONV = 3
NORM_EPS = 1e-6
NEG = -1e30
BIG = 1e9

IN_SIZES = (SSD_INNER, SSD_XBC, SSD_HEADS, NSA_INNER, NSA_KV, NSA_GATES, MEM_INNER, N_BRANCH * D_MODEL)

LANES = 128
BF16_SUBLANES = 16
VMEM_LIMIT = 56 * 1024 * 1024


def _bf16_terms(x, n):
    terms, rest = [], np.float32(x)
    for _ in range(n):
        t = np.float32(np.asarray(rest).astype(jnp.bfloat16))
        terms.append(float(t))
        rest = np.float32(rest - t)
    return tuple(terms)


LOG2E = float(np.float32(np.log2(np.e)))
LOG2E_TERMS = _bf16_terms(LOG2E, 3)

ROW_TILE = 512
NSA_TQ = 256
NSA_TK = 512
SSD_CHUNKS_PER_STEP = 8
FFN_CHUNK = 256
FFN_DOWN_GROUP = 6


def _dot(a, b):
    return jnp.dot(a, b, preferred_element_type=F32)


def _dot_nt(a, b):
    return lax.dot_general(a, b, (((1,), (1,)), ((), ())), preferred_element_type=F32)


def _split3(x):
    hi = x.astype(BF16)
    r1 = x - hi.astype(F32)
    mid = r1.astype(BF16)
    lo = (r1 - mid.astype(F32)).astype(BF16)
    return hi, mid, lo


def _dot3(x, w):
    hi, mid, lo = _split3(x)
    return _dot(hi, w) + _dot(mid, w) + _dot(lo, w)


def _silu(x):
    return x * jax.nn.sigmoid(x)


def _rms_rows(x, gain):
    return x * lax.rsqrt(jnp.mean(x * x, axis=-1, keepdims=True) + NORM_EPS) * gain


def _const_spec(shape):
    nd = len(shape)
    return pl.BlockSpec(shape, lambda *_: (0,) * nd, pipeline_mode=pl.Buffered(1))


def _params(sem, flags=None):
    return pltpu.CompilerParams(dimension_semantics=sem, vmem_limit_bytes=VMEM_LIMIT, flags=flags)


def _head_rms(v, gain, d):
    if d == LANES:
        ss = jnp.sum(v * v, axis=-1, keepdims=True)
        return v * lax.rsqrt(ss / d + NORM_EPS) * gain
    lane = lax.broadcasted_iota(jnp.int32, v.shape, 1)
    head = lane < d
    ss = jnp.sum(jnp.where(head, v * v, 0.0), axis=-1, keepdims=True)
    return jnp.where(head, v * lax.rsqrt(ss / d + NORM_EPS) * gain, v)


def _pair_rms(v, gain, d):
    lo = lax.broadcasted_iota(jnp.int32, v.shape, 1) < d
    sq = v * v
    ss_lo = jnp.sum(jnp.where(lo, sq, 0.0), axis=-1, keepdims=True)
    ss_hi = jnp.sum(jnp.where(lo, 0.0, sq), axis=-1, keepdims=True)
    return v * jnp.where(lo, lax.rsqrt(ss_lo / d + NORM_EPS), lax.rsqrt(ss_hi / d + NORM_EPS)) * gain


def _in_proj_kernel(x_ref, g_ref, wz, wxbc, wsm, wq, wkvc, wkv, wqm, wgm, qn_ref, kn_ref, mqn_ref,
                    z_o, xbc_o, sm_o, q_o, kvc_o, kv_o, qm_o, gm_o):
    xn = _rms_rows(x_ref[...], g_ref[...]).astype(BF16)
    z_o[...] = _dot(xn, wz[...]).astype(z_o.dtype)
    xbc_o[...] = _dot(xn, wxbc[...]).astype(xbc_o.dtype)
    sm_o[...] = _dot(xn, wsm[...])
    gm_o[...] = _dot(xn, wgm[...]).astype(gm_o.dtype)
    kvc_o[...] = _dot(xn, wkvc[...]).astype(kvc_o.dtype)
    q = _dot(xn, wq[...])
    for p in range(NSA_HEADS // 2):
        sl = slice(p * LANES, (p + 1) * LANES)
        q_o[:, sl] = _pair_rms(q[:, sl], qn_ref[...], NSA_HEAD_DIM).astype(q_o.dtype)
    kv = _dot(xn, wkv[...])
    for j in range(2 * NSA_KV_GROUPS):
        sl = slice(j * LANES, (j + 1) * LANES)
        kv_o[:, sl] = _head_rms(kv[:, sl], kn_ref[j % 2:j % 2 + 1, :], NSA_HEAD_DIM).astype(kv_o.dtype)
    qm = _dot(xn, wqm[...])
    for h in range(MEM_HEADS):
        sl = slice(h * LANES, (h + 1) * LANES)
        qm_o[:, sl] = _head_rms(qm[:, sl], mqn_ref[...], MEM_HEAD_DIM).astype(qm_o.dtype)


def _in_proj(x2, norm_mix, w_in, nsa_q_norm, nsa_k_norm, mem_q_norm):
    T = x2.shape[0]
    TM = ROW_TILE
    o = [0]
    for s in IN_SIZES:
        o.append(o[-1] + s)
    wz = w_in[:, o[0]:o[1]]
    wxbc = w_in[:, o[1]:o[2]]
    wq = w_in[:, o[3]:o[4]]
    wkv5 = w_in[:, o[4]:o[5]].reshape(D_MODEL, 3, 2, NSA_KV_GROUPS, NSA_HEAD_DIM)
    wkvc = wkv5[:, 0].transpose(0, 2, 1, 3).reshape(D_MODEL, NSA_KV_GROUPS * LANES)
    wkv = wkv5[:, 1:].transpose(0, 3, 1, 2, 4).reshape(D_MODEL, NSA_KV_GROUPS * 2 * LANES)
    wgn = w_in[:, o[5]:o[6]].reshape(D_MODEL, 3, NSA_KV_GROUPS, NSA_HG).transpose(0, 2, 1, 3).reshape(D_MODEL, NSA_GATES)
    wsm = jnp.pad(jnp.concatenate([w_in[:, o[2]:o[3]], wgn], axis=1), ((0, 0), (0, LANES - SSD_HEADS - NSA_GATES)))
    wqm = w_in[:, o[6]:o[7]]
    wgm = w_in[:, o[7]:o[8]]
    weights = [w.astype(BF16) for w in (wz, wxbc, wsm, wq, wkvc, wkv, wqm, wgm)]
    qn = jnp.tile(nsa_q_norm * (NSA_HEAD_DIM ** -0.5 * LOG2E), 2).reshape(1, LANES)
    kn = jnp.concatenate([nsa_k_norm[1:3], jnp.ones((2, LANES - NSA_HEAD_DIM), F32)], axis=1)
    mqn = (mem_q_norm * (MEM_HEAD_DIM ** -0.5 * LOG2E)).reshape(1, LANES)
    widths = [w.shape[1] for w in weights]
    dtypes = [BF16, BF16, F32, BF16, BF16, BF16, BF16, BF16]
    out_shape = [jax.ShapeDtypeStruct((T, w), dt) for w, dt in zip(widths, dtypes)]
    row = lambda w: pl.BlockSpec((TM, w), lambda i: (i, 0))
    return pl.pallas_call(
        _in_proj_kernel,
        grid=(T // TM,),
        in_specs=[row(D_MODEL), _const_spec((1, D_MODEL))] + [_const_spec(w.shape) for w in weights]
        + [_const_spec((1, LANES)), _const_spec((2, LANES)), _const_spec((1, LANES))],
        out_specs=[row(w) for w in widths],
        out_shape=out_shape,
        compiler_params=_params(("parallel",)),
        name="in_proj",
    )(x2, norm_mix.reshape(1, D_MODEL), *weights, qn, kn, mqn)


def _softplus(x):
    return jnp.maximum(x, 0.0) + jnp.log1p(jnp.exp(-jnp.abs(x)))


def _ssd_kernel(z_ref, xbc_ref, halo_ref, dt_ref, cw_ref, cb_ref, dtb_ref, alog_ref, dexp_ref, ng_ref, r_ref,
                o_ref, state):
    L = SSD_CHUNK
    HALO = BF16_SUBLANES
    c = pl.program_id(1)

    @pl.when(c == 0)
    def _():
        state[...] = jnp.zeros_like(state)

    for sc in range(z_ref.shape[0] // L):
        rows = slice(sc * L, (sc + 1) * L)
        if sc == 0:
            halo = jnp.where(c == 0, jnp.zeros_like(halo_ref[...]), halo_ref[...])
        else:
            halo = xbc_ref[sc * L - HALO:sc * L, :]
        o_ref[rows, :] = _ssd_chunk(z_ref[rows, :], xbc_ref[rows, :], halo, dt_ref[rows, :], cw_ref, cb_ref, dtb_ref,
                                    alog_ref, dexp_ref, ng_ref, r_ref, state).astype(o_ref.dtype)


def _ssd_chunk(z, xbc, halo, dt_raw, cw_ref, cb_ref, dtb_ref, alog_ref, dexp_ref, ng_ref, r_ref, state):
    L = SSD_CHUNK
    HALO = BF16_SUBLANES
    xe = jnp.concatenate([halo, xbc], axis=0)
    src = lax.broadcasted_iota(jnp.int32, (L, HALO + L), 1) - lax.broadcasted_iota(jnp.int32, (L, HALO + L), 0)
    acc = cb_ref[...] + cw_ref[SSD_CONV - 1:SSD_CONV, :] * xbc.astype(F32)
    for k in range(SSD_CONV - 1):
        shift = jnp.where(src == HALO - (SSD_CONV - 1) + k, 1.0, 0.0).astype(BF16)
        acc = acc + cw_ref[k:k + 1, :] * _dot(shift, xe)
    xa = _silu(acc)
    xs = xa[:, :SSD_INNER]

    lane = lax.broadcasted_iota(jnp.int32, (L, LANES), 1)
    rowi = lax.broadcasted_iota(jnp.int32, (L, L), 0)
    coli = lax.broadcasted_iota(jnp.int32, (L, L), 1)
    tril = rowi >= coli
    tril_w = jnp.where(tril, 1.0, 0.0).astype(BF16)

    head_lane = lane < SSD_HEADS
    dt = jnp.where(head_lane, _softplus(dt_raw + dtb_ref[...]), 0.0)
    d_a = dt * jnp.where(head_lane[0:1], -jnp.exp(alog_ref[...]) * LOG2E, 0.0)
    cs = sum(_dot(tril_w, part) for part in _split3(d_a))
    cs_t = cs.T
    r = r_ref[...]
    dt_x = _dot3(dt, r)
    cs_x = _dot3(cs, r)
    ecs_x = jnp.exp2(cs_x)
    decay_x = jnp.exp2(cs_x[L - 1:L, :] - cs_x)

    xdt = xs * dt_x
    xdtd = (xdt * decay_x).astype(BF16)
    xdt16 = xdt.astype(BF16)
    y_skip = xs * dexp_ref[...]
    lane_lo = lane < SSD_HEAD_DIM

    hpg = SSD_HEADS // SSD_GROUPS
    gw = hpg * SSD_HEAD_DIM
    y_blocks = []
    for g in range(SSD_GROUPS):
        b_g = xa[:, SSD_INNER + g * SSD_STATE:SSD_INNER + (g + 1) * SSD_STATE]
        c_g = xa[:, SSD_INNER + (SSD_GROUPS + g) * SSD_STATE:SSD_INNER + (SSD_GROUPS + g + 1) * SSD_STATE]
        b16 = b_g.astype(BF16)
        c16 = c_g.astype(BF16)
        cb = _dot_nt(c16, b16)
        st = state[:, g * gw:(g + 1) * gw]
        y_off = _dot(c16, st.astype(BF16)) * ecs_x[:, g * gw:(g + 1) * gw]
        bt16 = b_g.T.astype(BF16)
        state[:, g * gw:(g + 1) * gw] = (st * ecs_x[L - 1:L, g * gw:(g + 1) * gw]
                                         + _dot(bt16, xdtd[:, g * gw:(g + 1) * gw]))
        for hp in range(hpg // 2):
            col = g * gw + hp * LANES
            xp = xdt16[:, col:col + LANES]
            pair = []
            for j in range(2):
                h = g * hpg + hp * 2 + j
                seg = cs[:, h:h + 1] - cs_t[h:h + 1, :]
                lm = jnp.where(tril, jnp.exp2(jnp.where(tril, seg, 0.0)), 0.0)
                pair.append(_dot((cb * lm).astype(BF16), xp))
            y_diag = jnp.where(lane_lo, pair[0], pair[1])
            y_blocks.append(y_diag + y_off[:, hp * LANES:(hp + 1) * LANES] + y_skip[:, col:col + LANES])
    y = jnp.concatenate(y_blocks, axis=1)
    yz = y * _silu(z.astype(F32))
    return _rms_rows(yz, ng_ref[...])


def _ssd(z, xbc, dt, conv_w, conv_b, dt_bias, a_log, d_skip, norm_g):
    B, S, _ = z.shape
    L = SSD_CHUNK
    HALO = BF16_SUBLANES
    pad = lambda v: jnp.pad(v, (0, LANES - SSD_HEADS)).reshape(1, LANES)
    d_exp = jnp.repeat(d_skip, SSD_HEAD_DIM).reshape(1, SSD_INNER)
    expand = (jnp.arange(LANES)[:, None] == (jnp.arange(SSD_INNER)[None, :] // SSD_HEAD_DIM)).astype(BF16)
    step = SSD_CHUNKS_PER_STEP * SSD_CHUNK
    blk = lambda w: pl.BlockSpec((None, step, w), lambda b, c: (b, c, 0))
    halo_spec = pl.BlockSpec((None, HALO, SSD_XBC), lambda b, c: (b, jnp.maximum(c * (step // HALO) - 1, 0), 0))
    return pl.pallas_call(
        _ssd_kernel,
        grid=(B, S // step),
        in_specs=[blk(SSD_INNER), blk(SSD_XBC), halo_spec, blk(LANES),
                  _const_spec((SSD_CONV, SSD_XBC)), _const_spec((1, SSD_XBC)), _const_spec((1, LANES)),
                  _const_spec((1, LANES)), _const_spec((1, SSD_INNER)), _const_spec((1, SSD_INNER)),
                  _const_spec((LANES, SSD_INNER))],
        out_specs=blk(SSD_INNER),
        out_shape=jax.ShapeDtypeStruct((B, S, SSD_INNER), BF16),
        scratch_shapes=[pltpu.VMEM((SSD_STATE, SSD_INNER), F32)],
        compiler_params=_params(("parallel", "arbitrary")),
        name="ssd",
    )(z, xbc, xbc, dt, conv_w, conv_b.reshape(1, SSD_XBC), pad(dt_bias), pad(a_log), d_exp,
      norm_g.reshape(1, SSD_INNER), expand)


def _cmp_kernel(kvc_ref, pea_ref, peb_ref, w1a_ref, w1b_ref, w2_ref, kn_ref, o_ref):
    n_chunk = kvc_ref.shape[0]
    per = NSA_CMP_STRIDE
    row = lax.broadcasted_iota(jnp.int32, (n_chunk, LANES), 0)
    for g in range(NSA_KV_GROUPS):
        x = jnp.concatenate([kvc_ref[:, (NSA_KV_GROUPS * l + g) * LANES:(NSA_KV_GROUPS * l + g + 1) * LANES]
                             for l in range(per)], axis=1).astype(F32)
        a = _dot((x + pea_ref[...]).astype(BF16), w1a_ref[...])
        b = _dot((x + peb_ref[...]).astype(BF16), w1b_ref[...])
        hid = _silu(a + pltpu.roll(b, n_chunk - 1, axis=0))
        cmp = _dot(hid.astype(BF16), w2_ref[...])
        cmp = _head_rms(cmp, kn_ref[...], NSA_HEAD_DIM)
        o_ref[g] = jnp.where(row < n_chunk - 1, cmp, 0.0).astype(o_ref.dtype)


def _cmp(kvc, nsa_cmp_pe, nsa_cmp_w1, nsa_cmp_w2, k_norm0):
    B, S, _ = kvc.shape
    n_chunk = S // NSA_CMP_STRIDE
    per = NSA_CMP_STRIDE
    dk = NSA_HEAD_DIM
    kvc_r = kvc.reshape(B, n_chunk, per * NSA_KV_GROUPS * LANES)
    pe = jnp.concatenate([nsa_cmp_pe[0], nsa_cmp_pe[1]], axis=1)
    pea = pe[:per].reshape(1, per * LANES)
    peb = pe[per:].reshape(1, per * LANES)
    w1 = nsa_cmp_w1.reshape(2, NSA_CMP_LEN, dk, NSA_CMP_HIDDEN)
    zero = jnp.zeros((NSA_CMP_LEN, dk, NSA_CMP_HIDDEN), F32)
    w1bd = jnp.concatenate([jnp.concatenate([w1[0], zero], axis=2), jnp.concatenate([zero, w1[1]], axis=2)], axis=1)
    w1a = w1bd[:per].reshape(per * LANES, 2 * NSA_CMP_HIDDEN).astype(BF16)
    w1b = w1bd[per:].reshape(per * LANES, 2 * NSA_CMP_HIDDEN).astype(BF16)
    z2 = jnp.zeros((NSA_CMP_HIDDEN, dk), F32)
    w2bd = jnp.concatenate([jnp.concatenate([nsa_cmp_w2[0], z2], axis=1),
                            jnp.concatenate([z2, nsa_cmp_w2[1]], axis=1)], axis=0).astype(BF16)
    kn = jnp.concatenate([k_norm0, jnp.ones((LANES - dk,), F32)]).reshape(1, LANES)
    return pl.pallas_call(
        _cmp_kernel,
        grid=(B,),
        in_specs=[pl.BlockSpec((None, n_chunk, per * NSA_KV_GROUPS * LANES), lambda b: (b, 0, 0)),
                  _const_spec(pea.shape), _const_spec(peb.shape), _const_spec(w1a.shape), _const_spec(w1b.shape),
                  _const_spec(w2bd.shape), _const_spec(kn.shape)],
        out_specs=pl.BlockSpec((None, NSA_KV_GROUPS, n_chunk, LANES), lambda b: (b, 0, 0, 0)),
        out_shape=jax.ShapeDtypeStruct((B, NSA_KV_GROUPS, n_chunk, LANES), BF16),
        compiler_params=_params(("parallel",)),
        name="nsa_cmp",
    )(kvc_r, pea, peb, w1a, w1b, w2bd, kn)


def _top_blocks(imp, jt):
    n_slc, TQ = imp.shape
    blk_q = lax.broadcasted_iota(jnp.int32, (n_slc, TQ), 0)
    imp = jnp.where(blk_q == 0, BIG, imp)
    imp = jnp.where(blk_q == jt, BIG, imp)
    imp = jnp.where(blk_q == jt - 1, BIG, imp)
    imp = jnp.where(blk_q > jt, NEG, imp)
    SUB = 8
    row8 = lax.broadcasted_iota(jnp.int32, (SUB, 1), 0)
    groups = [imp[k * SUB:(k + 1) * SUB] for k in range(n_slc // SUB)]
    ranks = [jnp.zeros((SUB, TQ), F32) for _ in groups]
    for i in range(n_slc):
        vi = jnp.broadcast_to(imp[i:i + 1, :], (SUB, TQ))
        for k, grp in enumerate(groups):
            if k * SUB > i:
                ahead = jnp.where(vi >= grp, 1.0, 0.0)
            elif k * SUB + SUB - 1 < i:
                ahead = jnp.where(vi > grp, 1.0, 0.0)
            else:
                ahead = jnp.where(row8 + k * SUB > i, jnp.where(vi >= grp, 1.0, 0.0), jnp.where(vi > grp, 1.0, 0.0))
            ranks[k] = ranks[k] + ahead
    rank = jnp.concatenate(ranks, axis=0)
    return jnp.where(rank < NSA_TOP_N, jnp.where(blk_q <= jt, 1.0, 0.0), 0.0)


def _nsa_kernel(q_ref, kvc_ref, kv_ref, gate_ref, ov_ref, o_ref, k2c, vct, k2s, vst, k2w, vwt, s_a, s_b, m_st, acc_st,
                sw_buf, *, seq):
    TQ, TK, W = NSA_TQ, NSA_TK, NSA_WINDOW
    G, HG, DK = NSA_KV_GROUPS, NSA_HG, NSA_HEAD_DIM
    PAIR = 2
    n_cmp_pad = kvc_ref.shape[1]
    n_slc = seq // NSA_SLC_BLOCK
    t0 = pl.program_id(1) * TQ
    slopes = [[2.0 ** -(g * HG + h + 1) for h in range(HG)] for g in range(G)]

    @pl.when(pl.program_id(1) == 0)
    def _build_keys():
        lane = lax.broadcasted_iota(jnp.int32, (LANES, LANES), 1)
        row = lax.broadcasted_iota(jnp.int32, (LANES, LANES), 0)

        def with_pos(kv, pos):
            rel = lane - DK
            feat = jnp.where(rel % 2 == 0, pos // NSA_SLC_BLOCK, pos % NSA_SLC_BLOCK)
            feat = jnp.where(rel < 2 * len(LOG2E_TERMS), feat, 0).astype(F32)
            return jnp.where(rel < 0, kv, feat).astype(BF16)

        ones_row = jnp.where(lax.broadcasted_iota(jnp.int32, (BF16_SUBLANES, LANES), 0) == 0, 1.0, 0.0)

        def values_t(kv):
            return jnp.concatenate([kv.T[DK:], ones_row], axis=0).astype(BF16)

        for g in range(G):
            for c in range(n_cmp_pad // LANES):
                rows = slice(c * LANES, (c + 1) * LANES)
                kv = kvc_ref[g, rows, :].astype(F32)
                k2c[g, rows, :] = with_pos(kv, (row + c * LANES) * NSA_CMP_STRIDE + (NSA_CMP_LEN - 1))
                vct[g, :, rows] = values_t(kv)

        def chunk(c, carry):
            r0 = pl.multiple_of(c * LANES, LANES)
            pos = row + r0
            one_hot = jnp.where(lane == pos // NSA_SLC_BLOCK, 1.0, 0.0).astype(BF16)
            for g in range(G):
                kv = kv_ref[pl.ds(r0, LANES), 2 * g * LANES:(2 * g + 1) * LANES].astype(F32)
                k2s[g, pl.ds(r0, LANES), 0:LANES] = with_pos(kv, pos)
                k2s[g, pl.ds(r0, LANES), LANES:2 * LANES] = one_hot
                vst[g, :, pl.ds(r0, LANES)] = values_t(kv)
                kvw = kv_ref[pl.ds(r0, LANES), (2 * g + 1) * LANES:(2 * g + 2) * LANES].astype(F32)
                k2w[g, pl.ds(r0, LANES), :] = with_pos(kvw, pos)
                vwt[g, :, pl.ds(r0, LANES)] = values_t(kvw)
            return carry

        lax.fori_loop(0, seq // LANES, chunk, 0)

    rel_q = lax.broadcasted_iota(jnp.int32, (1, LANES), 1) - DK
    lane_lo = lax.broadcasted_iota(jnp.int32, (TQ, LANES), 1) < DK
    q1 = [[None] * HG for _ in range(G)]
    for g in range(G):
        for h in range(HG):
            feat = jnp.zeros((1, LANES), F32)
            for i, term in enumerate(LOG2E_TERMS):
                feat = jnp.where(rel_q == 2 * i, NSA_SLC_BLOCK * slopes[g][h] * term, feat)
                feat = jnp.where(rel_q == 2 * i + 1, slopes[g][h] * term, feat)
            head = g * HG + h
            qpair = q_ref[:, (head // 2) * LANES:(head // 2 + 1) * LANES].astype(F32)
            if head % 2:
                qpair = pltpu.roll(qpair, DK, axis=1)
            q1[g][h] = (jnp.where(lane_lo, qpair, 0.0) + feat).astype(BF16)
    chains = [(g, hp) for g in range(G) for hp in range(HG // PAIR)]
    q1c = {(g, hp): jnp.concatenate(q1[g][hp * PAIR:(hp + 1) * PAIR], axis=0) for g, hp in chains}
    tq = t0 + lax.broadcasted_iota(jnp.int32, (1, TQ), 1)
    jt = tq // NSA_SLC_BLOCK

    def tile_pair(mask):
        return jnp.concatenate([mask] * PAIR, axis=1)

    cend = lax.broadcasted_iota(jnp.int32, (n_cmp_pad, 1), 0) * NSA_CMP_STRIDE + (NSA_CMP_LEN - 1)
    valid = (tq - cend) >= 0
    o_cmp = {}
    psum = [jnp.zeros((n_cmp_pad, TQ), F32) for _ in range(G)]
    sc_all = [_dot_nt(k2c[g], q1c[g, hp]) for g, hp in chains]
    band = W + TQ
    w0 = pl.multiple_of(jnp.maximum(t0 - W, 0), TQ)
    dw = tq - (w0 + lax.broadcasted_iota(jnp.int32, (band, 1), 0))
    in_win = tile_pair(jnp.where(dw >= 0, jnp.where(dw < W, 0.0, NEG), NEG))
    for ci, (g, hp) in enumerate(chains):
        sw_buf[ci] = _dot_nt(k2w[g, pl.ds(w0, band), :], q1c[g, hp])
    for (g, hp), sct in zip(chains, sc_all):
        p_list = []
        for j in range(PAIR):
            s = jnp.where(valid, sct[:, j * TQ:(j + 1) * TQ], NEG)
            m = jnp.max(s, axis=0, keepdims=True)
            p_list.append(jnp.where(valid, jnp.exp2(s - m), 0.0))
        pv = _dot(vct[g], jnp.concatenate(p_list, axis=1).astype(BF16))
        l = pv[DK:DK + 1]
        inv = 1.0 / jnp.where(l > 0.0, l, 1.0)
        o_cmp[g, hp] = pv[0:DK] * inv
        for j in range(PAIR):
            psum[g] = psum[g] + p_list[j] * inv[:, j * TQ:(j + 1) * TQ]
    q2c = {}
    for g in range(G):
        imp = sum(_dot(ov_ref[...], part) for part in _split3(psum[g]))
        sel = _top_blocks(imp[0:n_slc], jt)
        if n_slc < LANES:
            sel = jnp.concatenate([sel, jnp.zeros((LANES - n_slc, TQ), F32)], axis=0)
        sel_neg = jnp.where(sel.T > 0.5, 0.0, -(2.0 ** 100)).astype(BF16)
        for hp in range(HG // PAIR):
            q2c[g, hp] = jnp.concatenate([jnp.concatenate([q1[g][hp * PAIR + j], sel_neg], axis=1)
                                          for j in range(PAIR)], axis=0)

    def slc_scores(kt, buf):
        k0 = pl.multiple_of(kt * TK, TK)
        for ci, (g, hp) in enumerate(chains):
            buf[ci] = _dot_nt(k2s[g, pl.ds(k0, TK), :], q2c[g, hp])

    def slc_softmax(kt, buf, masked):
        k0 = pl.multiple_of(kt * TK, TK)
        if masked:
            kpos = k0 + lax.broadcasted_iota(jnp.int32, (TK, 1), 0)
            causal_add = tile_pair(jnp.where(kpos <= tq, 0.0, NEG))
        for ci, (g, hp) in enumerate(chains):
            s = buf[ci]
            if masked:
                s = s + causal_add
            m = m_st[ci]
            m_new = jnp.maximum(m, jnp.max(s, axis=0, keepdims=True))
            p = jnp.exp2(s - m_new)
            acc_st[ci] = jnp.exp2(m - m_new) * acc_st[ci] + _dot(vst[g, :, pl.ds(k0, TK)], p.astype(BF16))
            m_st[ci] = m_new

    n_past = t0 // TK
    m_st[...] = jnp.full(m_st.shape, NEG, F32)
    acc_st[...] = jnp.zeros(acc_st.shape, F32)
    slc_scores(0, s_a)

    def slc_pair(j, carry):
        slc_scores(2 * j + 1, s_b)
        slc_softmax(2 * j, s_a, False)
        slc_scores(2 * j + 2, s_a)
        slc_softmax(2 * j + 1, s_b, False)
        return carry

    o_win = {}
    for ci, (g, hp) in enumerate(chains):
        sw = sw_buf[ci] + in_win
        pw = jnp.exp2(sw - jnp.max(sw, axis=0, keepdims=True))
        pv = _dot(vwt[g, :, pl.ds(w0, band)], pw.astype(BF16))
        o_win[g, hp] = pv[0:DK] / pv[DK:DK + 1]

    lax.fori_loop(0, n_past // 2, slc_pair, 0)

    @pl.when(n_past % 2 == 1)
    def _odd_tail():
        slc_scores(n_past, s_b)
        slc_softmax(n_past - 1, s_a, False)
        slc_softmax(n_past, s_b, True)

    @pl.when(n_past % 2 == 0)
    def _even_tail():
        slc_softmax(n_past, s_a, True)

    o_slc = {ch: acc_st[ci, 0:DK] / acc_st[ci, DK:DK + 1] for ci, ch in enumerate(chains)}

    comb = []
    gt = jax.nn.sigmoid(gate_ref[...]).T
    for g in range(G):
        for h in range(HG):
            ch = (g, h // PAIR)
            cols = slice((h % PAIR) * TQ, (h % PAIR + 1) * TQ)
            gate = [gt[r:r + 1] for r in (SSD_HEADS + (g * 3 + br) * HG + h for br in range(3))]
            comb.append(gate[0] * o_cmp[ch][:, cols] + gate[1] * o_slc[ch][:, cols]
                        + gate[2] * o_win[ch][:, cols])
    o_ref[...] = jnp.concatenate(comb, axis=0).T.astype(o_ref.dtype)


def _nsa(q, kvcmp, kv, gates, B, S):
    TQ = NSA_TQ
    G = NSA_KV_GROUPS
    assert S % (NSA_CMP_STRIDE * LANES) == 0 and S % NSA_TK == 0 and NSA_TK % TQ == 0 and S >= NSA_WINDOW + TQ
    n_cmp_pad = S // NSA_CMP_STRIDE
    n_slc = S // NSA_SLC_BLOCK
    cs = jnp.arange(n_cmp_pad) * NSA_CMP_STRIDE
    ce = cs + NSA_CMP_LEN - 1
    ss = jnp.arange(LANES) * NSA_SLC_BLOCK
    overlap = jnp.clip(jnp.minimum(ce[None, :], ss[:, None] + NSA_SLC_BLOCK - 1)
                       - jnp.maximum(cs[None, :], ss[:, None]) + 1, 0).astype(F32) / NSA_CMP_LEN
    valid_pair = (jnp.arange(LANES)[:, None] < n_slc) & (jnp.arange(n_cmp_pad)[None, :] < n_cmp_pad - 1)
    ov_t = jnp.where(valid_pair, overlap, 0.0).astype(BF16)
    q3 = q.reshape(B, S, NSA_INNER)
    kv3 = kv.reshape(B, S, G * 2 * LANES)
    g3 = gates.reshape(B, S, LANES)
    vt_rows = NSA_HEAD_DIM + BF16_SUBLANES
    n_chains = NSA_HEADS // 2
    return pl.pallas_call(
        functools.partial(_nsa_kernel, seq=S),
        grid=(B, S // TQ),
        in_specs=[pl.BlockSpec((None, TQ, NSA_INNER), lambda b, i: (b, i, 0)),
                  pl.BlockSpec((None, G, n_cmp_pad, LANES), lambda b, i: (b, 0, 0, 0)),
                  pl.BlockSpec((None, S, G * 2 * LANES), lambda b, i: (b, 0, 0)),
                  pl.BlockSpec((None, TQ, LANES), lambda b, i: (b, i, 0)),
                  _const_spec(ov_t.shape)],
        out_specs=pl.BlockSpec((None, TQ, NSA_INNER), lambda b, i: (b, i, 0)),
        out_shape=jax.ShapeDtypeStruct((B, S, NSA_INNER), BF16),
        scratch_shapes=[pltpu.VMEM((G, n_cmp_pad, LANES), BF16), pltpu.VMEM((G, vt_rows, n_cmp_pad), BF16),
                        pltpu.VMEM((G, S, 2 * LANES), BF16), pltpu.VMEM((G, vt_rows, S), BF16),
                        pltpu.VMEM((G, S, LANES), BF16), pltpu.VMEM((G, vt_rows, S), BF16),
                        pltpu.VMEM((n_chains, NSA_TK, 2 * TQ), F32), pltpu.VMEM((n_chains, NSA_TK, 2 * TQ), F32),
                        pltpu.VMEM((n_chains, 1, 2 * TQ), F32), pltpu.VMEM((n_chains, vt_rows, 2 * TQ), F32),
                        pltpu.VMEM((n_chains, NSA_WINDOW + TQ, 2 * TQ), F32)],
        compiler_params=_params(("parallel", "arbitrary")),
        name="nsa_attn",
    )(q3, kvcmp, kv3, g3, ov_t)


def _mem_kv_kernel(mem_ref, g_ref, w_ref, kn_ref, k_o, v_o):
    mn = _rms_rows(mem_ref[...], g_ref[...]).astype(BF16)
    kv = _dot(mn, w_ref[...])
    for h in range(MEM_HEADS):
        sl = slice(h * LANES, (h + 1) * LANES)
        k_o[:, sl] = _head_rms(kv[:, sl], kn_ref[...], MEM_HEAD_DIM).astype(k_o.dtype)
    v_o[...] = kv[:, MEM_INNER:].astype(v_o.dtype)


def _mem_kv(mem, mem_norm, w_mem_kv, mem_k_norm):
    B, M, _ = mem.shape
    spec_o = pl.BlockSpec((None, M, MEM_INNER), lambda b: (b, 0, 0))
    return pl.pallas_call(
        _mem_kv_kernel,
        grid=(B,),
        in_specs=[pl.BlockSpec((None, M, D_MODEL), lambda b: (b, 0, 0)), _const_spec((1, D_MODEL)),
                  _const_spec((D_MODEL, 2 * MEM_INNER)), _const_spec((1, LANES))],
        out_specs=[spec_o, spec_o],
        out_shape=[jax.ShapeDtypeStruct((B, M, MEM_INNER), BF16)] * 2,
        compiler_params=_params(("parallel",)),
        name="mem_kv",
    )(mem, mem_norm.reshape(1, D_MODEL), w_mem_kv.astype(BF16), mem_k_norm.reshape(1, LANES))


def _merge_kernel(x_ref, ssd_ref, nsa_ref, qm_ref, gm_ref, km_ref, vm_ref, wso, wno, wmo, wout, h_o):
    heads = [slice(h * LANES, (h + 1) * LANES) for h in range(MEM_HEADS)]
    scores = [_dot_nt(qm_ref[:, sl], km_ref[:, sl]) for sl in heads]
    y_ssd = _dot(ssd_ref[...], wso[...])
    y_nsa = _dot(nsa_ref[...], wno[...])
    o_mem = []
    for sl, s in zip(heads, scores):
        p = jnp.exp2(s - jnp.max(s, axis=-1, keepdims=True))
        o = _dot(p.astype(BF16), vm_ref[:, sl]) / jnp.sum(p, axis=-1, keepdims=True)
        o_mem.append(o.astype(BF16))
    y_mem = _dot(jnp.concatenate(o_mem, axis=1), wmo[...])
    gate = lambda j: jax.nn.sigmoid(gm_ref[:, j * D_MODEL:(j + 1) * D_MODEL].astype(F32))
    mixed = gate(0) * y_ssd + gate(1) * y_nsa + gate(2) * y_mem
    h_o[...] = x_ref[...] + _dot(mixed.astype(BF16), wout[...])


def _merge(x2, ssd_n, nsa_o, q_mem, g_merge, k_mem, v_mem, w_ssd_o, w_nsa_o, w_mem_o, w_out, S):
    T = x2.shape[0]
    TM = ROW_TILE
    M = k_mem.shape[1]
    per_batch = S // TM
    row = lambda w: pl.BlockSpec((TM, w), lambda i: (i, 0))
    mem_spec = pl.BlockSpec((None, M, MEM_INNER), lambda i: (i // per_batch, 0, 0))
    weights = [w.astype(BF16) for w in (w_ssd_o, w_nsa_o, w_mem_o, w_out)]
    return pl.pallas_call(
        _merge_kernel,
        grid=(T // TM,),
        in_specs=[row(D_MODEL), row(SSD_INNER), row(NSA_INNER), row(MEM_INNER), row(N_BRANCH * D_MODEL),
                  mem_spec, mem_spec] + [_const_spec(w.shape) for w in weights],
        out_specs=row(D_MODEL),
        out_shape=jax.ShapeDtypeStruct((T, D_MODEL), F32),
        compiler_params=_params(("parallel",)),
        name="merge",
    )(x2, ssd_n, nsa_o, q_mem, g_merge, k_mem, v_mem, *weights)


def _ffn_kernel(h_ref, halo_ref, g_ref, wup, cw_ref, cb_ref, wdown, o_ref, *, per_batch):
    TM = h_ref.shape[0]
    HALO = halo_ref.shape[0]
    CW = FFN_CHUNK
    first = pl.program_id(0) % per_batch == 0
    h = h_ref[...]
    halo_n = _rms_rows(halo_ref[...], g_ref[...]) * jnp.where(first, 0.0, 1.0)
    hn = jnp.concatenate([halo_n, _rms_rows(h, g_ref[...])], axis=0).astype(BF16)
    n_chunks = FFN_HIDDEN // CW

    def up_proj(c):
        return [_dot(hn, wup[:, half * FFN_HIDDEN + c * CW:half * FFN_HIDDEN + (c + 1) * CW]) for half in range(2)]

    acc = jnp.zeros((TM, D_MODEL), F32)
    ext = up_proj(0)
    pending = []
    for c in range(n_chunks):
        ext_next = up_proj(c + 1) if c + 1 < n_chunks else None
        conv = []
        for half in range(2):
            col = half * FFN_HIDDEN + c * CW
            u = jnp.broadcast_to(cb_ref[:, col:col + CW], (TM, CW))
            for k in range(FFN_CONV):
                start = HALO - (FFN_CONV - 1) + k
                u = u + cw_ref[k:k + 1, col:col + CW] * ext[half][start:start + TM]
            conv.append(u)
        pending.append((_silu(conv[0]) * conv[1]).astype(BF16))
        if len(pending) == FFN_DOWN_GROUP or c + 1 == n_chunks:
            lo = (c + 1 - len(pending)) * CW
            acc = acc + _dot(jnp.concatenate(pending, axis=1), wdown[lo:(c + 1) * CW, :])
            pending = []
        ext = ext_next
    o_ref[...] = h + acc


def _ffn(h2, norm_ffn, w_ffn_up, ffn_conv_w, ffn_conv_b, w_ffn_down, S):
    T = h2.shape[0]
    TM = ROW_TILE
    HALO = BF16_SUBLANES
    per_batch = S // TM
    return pl.pallas_call(
        functools.partial(_ffn_kernel, per_batch=per_batch),
        grid=(T // TM,),
        in_specs=[pl.BlockSpec((TM, D_MODEL), lambda i: (i, 0)),
                  pl.BlockSpec((HALO, D_MODEL), lambda i: (jnp.maximum(i * (TM // HALO) - 1, 0), 0)),
                  _const_spec((1, D_MODEL)), _const_spec((D_MODEL, 2 * FFN_HIDDEN)),
                  _const_spec((FFN_CONV, 2 * FFN_HIDDEN)), _const_spec((1, 2 * FFN_HIDDEN)),
                  _const_spec((FFN_HIDDEN, D_MODEL))],
        out_specs=pl.BlockSpec((TM, D_MODEL), lambda i: (i, 0)),
        out_shape=jax.ShapeDtypeStruct((T, D_MODEL), F32),
        compiler_params=_params(("parallel",)),
        name="ffn",
    )(h2, h2, norm_ffn.reshape(1, D_MODEL), w_ffn_up.astype(BF16), ffn_conv_w,
      ffn_conv_b.reshape(1, 2 * FFN_HIDDEN), w_ffn_down.astype(BF16))


def _layer(x, mem, norm_mix, w_in, ssd_conv_w, ssd_conv_b, ssd_dt_bias, ssd_a_log, ssd_d, ssd_norm, w_ssd_o,
           nsa_q_norm, nsa_k_norm, nsa_cmp_pe, nsa_cmp_w1, nsa_cmp_w2, w_nsa_o, mem_norm, w_mem_kv,
           mem_q_norm, mem_k_norm, w_mem_o, w_out, norm_ffn, w_ffn_up, ffn_conv_w, ffn_conv_b, w_ffn_down):
    B, S, D = x.shape
    x2 = x.reshape(B * S, D)
    z, xbc, small, q, kvc, kv, qm, gm = _in_proj(x2, norm_mix, w_in, nsa_q_norm, nsa_k_norm, mem_q_norm)
    ssd_n = _ssd(z.reshape(B, S, -1), xbc.reshape(B, S, -1), small.reshape(B, S, -1), ssd_conv_w, ssd_conv_b,
                 ssd_dt_bias, ssd_a_log, ssd_d, ssd_norm)
    kvcmp = _cmp(kvc.reshape(B, S, -1), nsa_cmp_pe, nsa_cmp_w1, nsa_cmp_w2, nsa_k_norm[0])
    nsa_o = _nsa(q, kvcmp, kv, small, B, S)
    k_mem, v_mem = _mem_kv(mem, mem_norm, w_mem_kv, mem_k_norm)
    h2 = _merge(x2, ssd_n.reshape(B * S, -1), nsa_o.reshape(B * S, -1), qm, gm, k_mem, v_mem,
                w_ssd_o, w_nsa_o, w_mem_o, w_out, S)
    out = _ffn(h2, norm_ffn, w_ffn_up, ffn_conv_w, ffn_conv_b, w_ffn_down, S)
    return out.reshape(B, S, D)


def kernel(x, mem, norm_mix, w_in, ssd_conv_w, ssd_conv_b, ssd_dt_bias, ssd_a_log, ssd_d, ssd_norm, w_ssd_o, nsa_q_norm, nsa_k_norm, nsa_cmp_pe, nsa_cmp_w1, nsa_cmp_w2, w_nsa_o, mem_norm, w_mem_kv, mem_q_norm, mem_k_norm, w_mem_o, w_out, norm_ffn, w_ffn_up, ffn_conv_w, ffn_conv_b, w_ffn_down):
    h = x
    for i in range(norm_mix.shape[0]):
        h = _layer(h, mem, norm_mix[i], w_in[i], ssd_conv_w[i], ssd_conv_b[i], ssd_dt_bias[i], ssd_a_log[i],
                   ssd_d[i], ssd_norm[i], w_ssd_o[i], nsa_q_norm[i], nsa_k_norm[i], nsa_cmp_pe[i],
                   nsa_cmp_w1[i], nsa_cmp_w2[i], w_nsa_o[i], mem_norm[i], w_mem_kv[i], mem_q_norm[i],
                   mem_k_norm[i], w_mem_o[i], w_out[i], norm_ffn[i], w_ffn_up[i], ffn_conv_w[i],
                   ffn_conv_b[i], w_ffn_down[i])
    return h
```

```python
import functools

import jax
import jax.numpy as jnp
import numpy as np
from jax import lax
from jax.experimental import pallas as pl
from jax.experimental.pallas import tpu as pltpu

F32 = jnp.float32
BF16 = jnp.bfloat16

D_MODEL = 1024
SSD_HEADS = 16
SSD_HEAD_DIM = 64
SSD_GROUPS = 2
SSD_STATE = 128
SSD_CONV = 4
SSD_CHUNK = 128
SSD_INNER = SSD_HEADS * SSD_HEAD_DIM
SSD_XBC = SSD_INNER + 2 * SSD_GROUPS * SSD_STATE
NSA_HEADS = 8
NSA_KV_GROUPS = 2
NSA_HG = NSA_HEADS // NSA_KV_GROUPS
NSA_HEAD_DIM = 64
NSA_CMP_LEN = 32
NSA_CMP_STRIDE = 16
NSA_CMP_HIDDEN = 256
NSA_SLC_BLOCK = 64
NSA_TOP_N = 16
NSA_WINDOW = 512
NSA_INNER = NSA_HEADS * NSA_HEAD_DIM
NSA_KV = 3 * 2 * NSA_KV_GROUPS * NSA_HEAD_DIM
NSA_GATES = 3 * NSA_HEADS
MEM_HEADS = 4
MEM_HEAD_DIM = 128
MEM_INNER = MEM_HEADS * MEM_HEAD_DIM
N_BRANCH = 3
FFN_HIDDEN = 2816
FFN_CONV = 3
NORM_EPS = 1e-6
NEG = -1e30
BIG = 1e9

IN_SIZES = (SSD_INNER, SSD_XBC, SSD_HEADS, NSA_INNER, NSA_KV, NSA_GATES, MEM_INNER, N_BRANCH * D_MODEL)

LANES = 128
BF16_SUBLANES = 16
VMEM_LIMIT = 56 * 1024 * 1024


def _bf16_terms(x, n):
    terms, rest = [], np.float32(x)
    for _ in range(n):
        t = np.float32(np.asarray(rest).astype(jnp.bfloat16))
        terms.append(float(t))
        rest = np.float32(rest - t)
    return tuple(terms)


LOG2E = float(np.float32(np.log2(np.e)))
LOG2E_TERMS = _bf16_terms(LOG2E, 3)

ROW_TILE = 512
NSA_TQ = 256
NSA_TK = 512
NSA_STEP_VARIANTS = 4
SSD_CHUNKS_PER_STEP = 8
FFN_CHUNK = 256
FFN_DOWN_GROUP = 6


def _dot(a, b):
    return jnp.dot(a, b, preferred_element_type=F32)


def _dot_nt(a, b):
    return lax.dot_general(a, b, (((1,), (1,)), ((), ())), preferred_element_type=F32)


def _split3(x):
    hi = x.astype(BF16)
    r1 = x - hi.astype(F32)
    mid = r1.astype(BF16)
    lo = (r1 - mid.astype(F32)).astype(BF16)
    return hi, mid, lo


def _dot3(x, w):
    hi, mid, lo = _split3(x)
    return _dot(hi, w) + _dot(mid, w) + _dot(lo, w)


def _silu(x):
    return x * jax.nn.sigmoid(x)


def _rms_rows(x, gain):
    return x * lax.rsqrt(jnp.mean(x * x, axis=-1, keepdims=True) + NORM_EPS) * gain


def _const_spec(shape):
    nd = len(shape)
    return pl.BlockSpec(shape, lambda *_: (0,) * nd, pipeline_mode=pl.Buffered(1))


def _params(sem, flags=None):
    return pltpu.CompilerParams(dimension_semantics=sem, vmem_limit_bytes=VMEM_LIMIT, flags=flags)


def _head_rms(v, gain, d):
    if d == LANES:
        ss = jnp.sum(v * v, axis=-1, keepdims=True)
        return v * lax.rsqrt(ss / d + NORM_EPS) * gain
    lane = lax.broadcasted_iota(jnp.int32, v.shape, 1)
    head = lane < d
    ss = jnp.sum(jnp.where(head, v * v, 0.0), axis=-1, keepdims=True)
    return jnp.where(head, v * lax.rsqrt(ss / d + NORM_EPS) * gain, v)


def _pair_rms(v, gain, d):
    lo = lax.broadcasted_iota(jnp.int32, v.shape, 1) < d
    sq = v * v
    ss_lo = jnp.sum(jnp.where(lo, sq, 0.0), axis=-1, keepdims=True)
    ss_hi = jnp.sum(jnp.where(lo, 0.0, sq), axis=-1, keepdims=True)
    return v * jnp.where(lo, lax.rsqrt(ss_lo / d + NORM_EPS), lax.rsqrt(ss_hi / d + NORM_EPS)) * gain


def _in_proj_kernel(x_ref, g_ref, wz, wxbc, wsm, wq, wkvc, wkv, wqm, wgm, qn_ref, kn_ref, mqn_ref,
                    z_o, xbc_o, sm_o, q_o, kvc_o, kv_o, qm_o, gm_o):
    xn = _rms_rows(x_ref[...], g_ref[...]).astype(BF16)
    z_o[...] = _dot(xn, wz[...]).astype(z_o.dtype)
    xbc_o[...] = _dot(xn, wxbc[...]).astype(xbc_o.dtype)
    sm_o[...] = _dot(xn, wsm[...])
    gm_o[...] = _dot(xn, wgm[...]).astype(gm_o.dtype)
    kvc_o[...] = _dot(xn, wkvc[...]).astype(kvc_o.dtype)
    q = _dot(xn, wq[...])
    for p in range(NSA_HEADS // 2):
        sl = slice(p * LANES, (p + 1) * LANES)
        q_o[:, sl] = _pair_rms(q[:, sl], qn_ref[...], NSA_HEAD_DIM).astype(q_o.dtype)
    kv = _dot(xn, wkv[...])
    for j in range(2 * NSA_KV_GROUPS):
        sl = slice(j * LANES, (j + 1) * LANES)
        kv_o[:, sl] = _head_rms(kv[:, sl], kn_ref[j % 2:j % 2 + 1, :], NSA_HEAD_DIM).astype(kv_o.dtype)
    qm = _dot(xn, wqm[...])
    for h in range(MEM_HEADS):
        sl = slice(h * LANES, (h + 1) * LANES)
        qm_o[:, sl] = _head_rms(qm[:, sl], mqn_ref[...], MEM_HEAD_DIM).astype(qm_o.dtype)


def _in_proj(x2, norm_mix, w_in, nsa_q_norm, nsa_k_norm, mem_q_norm):
    T = x2.shape[0]
    TM = ROW_TILE
    o = [0]
    for s in IN_SIZES:
        o.append(o[-1] + s)
    wz = w_in[:, o[0]:o[1]]
    wxbc = w_in[:, o[1]:o[2]]
    wq = w_in[:, o[3]:o[4]]
    wkv5 = w_in[:, o[4]:o[5]].reshape(D_MODEL, 3, 2, NSA_KV_GROUPS, NSA_HEAD_DIM)
    wkvc = wkv5[:, 0].transpose(0, 2, 1, 3).reshape(D_MODEL, NSA_KV_GROUPS * LANES)
    wkv = wkv5[:, 1:].transpose(0, 3, 1, 2, 4).reshape(D_MODEL, NSA_KV_GROUPS * 2 * LANES)
    wgn = w_in[:, o[5]:o[6]].reshape(D_MODEL, 3, NSA_KV_GROUPS, NSA_HG).transpose(0, 2, 1, 3).reshape(D_MODEL, NSA_GATES)
    wsm = jnp.pad(jnp.concatenate([w_in[:, o[2]:o[3]], wgn], axis=1), ((0, 0), (0, LANES - SSD_HEADS - NSA_GATES)))
    wqm = w_in[:, o[6]:o[7]]
    wgm = w_in[:, o[7]:o[8]]
    weights = [w.astype(BF16) for w in (wz, wxbc, wsm, wq, wkvc, wkv, wqm, wgm)]
    qn = jnp.tile(nsa_q_norm * (NSA_HEAD_DIM ** -0.5 * LOG2E), 2).reshape(1, LANES)
    kn = jnp.concatenate([nsa_k_norm[1:3], jnp.ones((2, LANES - NSA_HEAD_DIM), F32)], axis=1)
    mqn = (mem_q_norm * (MEM_HEAD_DIM ** -0.5 * LOG2E)).reshape(1, LANES)
    widths = [w.shape[1] for w in weights]
    dtypes = [BF16, BF16, F32, BF16, BF16, BF16, BF16, BF16]
    out_shape = [jax.ShapeDtypeStruct((T, w), dt) for w, dt in zip(widths, dtypes)]
    row = lambda w: pl.BlockSpec((TM, w), lambda i: (i, 0))
    return pl.pallas_call(
        _in_proj_kernel,
        grid=(T // TM,),
        in_specs=[row(D_MODEL), _const_spec((1, D_MODEL))] + [_const_spec(w.shape) for w in weights]
        + [_const_spec((1, LANES)), _const_spec((2, LANES)), _const_spec((1, LANES))],
        out_specs=[row(w) for w in widths],
        out_shape=out_shape,
        compiler_params=_params(("parallel",)),
        name="in_proj",
    )(x2, norm_mix.reshape(1, D_MODEL), *weights, qn, kn, mqn)


def _softplus(x):
    return jnp.maximum(x, 0.0) + jnp.log1p(jnp.exp(-jnp.abs(x)))


def _ssd_kernel(z_ref, xbc_ref, halo_ref, dt_ref, cw_ref, cb_ref, dtb_ref, alog_ref, dexp_ref, ng_ref, r_ref,
                o_ref, state):
    L = SSD_CHUNK
    HALO = BF16_SUBLANES
    c = pl.program_id(1)

    @pl.when(c == 0)
    def _():
        state[...] = jnp.zeros_like(state)

    for sc in range(z_ref.shape[0] // L):
        rows = slice(sc * L, (sc + 1) * L)
        if sc == 0:
            halo = jnp.where(c == 0, jnp.zeros_like(halo_ref[...]), halo_ref[...])
        else:
            halo = xbc_ref[sc * L - HALO:sc * L, :]
        o_ref[rows, :] = _ssd_chunk(z_ref[rows, :], xbc_ref[rows, :], halo, dt_ref[rows, :], cw_ref, cb_ref, dtb_ref,
                                    alog_ref, dexp_ref, ng_ref, r_ref, state).astype(o_ref.dtype)


def _ssd_chunk(z, xbc, halo, dt_raw, cw_ref, cb_ref, dtb_ref, alog_ref, dexp_ref, ng_ref, r_ref, state):
    L = SSD_CHUNK
    HALO = BF16_SUBLANES
    xe = jnp.concatenate([halo, xbc], axis=0)
    src = lax.broadcasted_iota(jnp.int32, (L, HALO + L), 1) - lax.broadcasted_iota(jnp.int32, (L, HALO + L), 0)
    acc = cb_ref[...] + cw_ref[SSD_CONV - 1:SSD_CONV, :] * xbc.astype(F32)
    for k in range(SSD_CONV - 1):
        shift = jnp.where(src == HALO - (SSD_CONV - 1) + k, 1.0, 0.0).astype(BF16)
        acc = acc + cw_ref[k:k + 1, :] * _dot(shift, xe)
    xa = _silu(acc)
    xs = xa[:, :SSD_INNER]

    lane = lax.broadcasted_iota(jnp.int32, (L, LANES), 1)
    rowi = lax.broadcasted_iota(jnp.int32, (L, L), 0)
    coli = lax.broadcasted_iota(jnp.int32, (L, L), 1)
    tril = rowi >= coli
    tril_w = jnp.where(tril, 1.0, 0.0).astype(BF16)

    head_lane = lane < SSD_HEADS
    dt = jnp.where(head_lane, _softplus(dt_raw + dtb_ref[...]), 0.0)
    d_a = dt * jnp.where(head_lane[0:1], -jnp.exp(alog_ref[...]) * LOG2E, 0.0)
    cs = sum(_dot(tril_w, part) for part in _split3(d_a))
    cs_t = cs.T
    r = r_ref[...]
    dt_x = _dot3(dt, r)
    cs_x = _dot3(cs, r)
    ecs_x = jnp.exp2(cs_x)
    decay_x = jnp.exp2(cs_x[L - 1:L, :] - cs_x)

    xdt = xs * dt_x
    xdtd = (xdt * decay_x).astype(BF16)
    xdt16 = xdt.astype(BF16)
    y_skip = xs * dexp_ref[...]
    lane_lo = lane < SSD_HEAD_DIM

    hpg = SSD_HEADS // SSD_GROUPS
    gw = hpg * SSD_HEAD_DIM
    y_blocks = []
    for g in range(SSD_GROUPS):
        b_g = xa[:, SSD_INNER + g * SSD_STATE:SSD_INNER + (g + 1) * SSD_STATE]
        c_g = xa[:, SSD_INNER + (SSD_GROUPS + g) * SSD_STATE:SSD_INNER + (SSD_GROUPS + g + 1) * SSD_STATE]
        b16 = b_g.astype(BF16)
        c16 = c_g.astype(BF16)
        cb = _dot_nt(c16, b16)
        st = state[:, g * gw:(g + 1) * gw]
        y_off = _dot(c16, st.astype(BF16)) * ecs_x[:, g * gw:(g + 1) * gw]
        bt16 = b_g.T.astype(BF16)
        state[:, g * gw:(g + 1) * gw] = (st * ecs_x[L - 1:L, g * gw:(g + 1) * gw]
                                         + _dot(bt16, xdtd[:, g * gw:(g + 1) * gw]))
        for hp in range(hpg // 2):
            col = g * gw + hp * LANES
            xp = xdt16[:, col:col + LANES]
            pair = []
            for j in range(2):
                h = g * hpg + hp * 2 + j
                seg = cs[:, h:h + 1] - cs_t[h:h + 1, :]
                lm = jnp.where(tril, jnp.exp2(jnp.where(tril, seg, 0.0)), 0.0)
                pair.append(_dot((cb * lm).astype(BF16), xp))
            y_diag = jnp.where(lane_lo, pair[0], pair[1])
            y_blocks.append(y_diag + y_off[:, hp * LANES:(hp + 1) * LANES] + y_skip[:, col:col + LANES])
    y = jnp.concatenate(y_blocks, axis=1)
    yz = y * _silu(z.astype(F32))
    return _rms_rows(yz, ng_ref[...])


def _ssd(z, xbc, dt, conv_w, conv_b, dt_bias, a_log, d_skip, norm_g):
    B, S, _ = z.shape
    L = SSD_CHUNK
    HALO = BF16_SUBLANES
    pad = lambda v: jnp.pad(v, (0, LANES - SSD_HEADS)).reshape(1, LANES)
    d_exp = jnp.repeat(d_skip, SSD_HEAD_DIM).reshape(1, SSD_INNER)
    expand = (jnp.arange(LANES)[:, None] == (jnp.arange(SSD_INNER)[None, :] // SSD_HEAD_DIM)).astype(BF16)
    step = SSD_CHUNKS_PER_STEP * SSD_CHUNK
    blk = lambda w: pl.BlockSpec((None, step, w), lambda b, c: (b, c, 0))
    halo_spec = pl.BlockSpec((None, HALO, SSD_XBC), lambda b, c: (b, jnp.maximum(c * (step // HALO) - 1, 0), 0))
    return pl.pallas_call(
        _ssd_kernel,
        grid=(B, S // step),
        in_specs=[blk(SSD_INNER), blk(SSD_XBC), halo_spec, blk(LANES),
                  _const_spec((SSD_CONV, SSD_XBC)), _const_spec((1, SSD_XBC)), _const_spec((1, LANES)),
                  _const_spec((1, LANES)), _const_spec((1, SSD_INNER)), _const_spec((1, SSD_INNER)),
                  _const_spec((LANES, SSD_INNER))],
        out_specs=blk(SSD_INNER),
        out_shape=jax.ShapeDtypeStruct((B, S, SSD_INNER), BF16),
        scratch_shapes=[pltpu.VMEM((SSD_STATE, SSD_INNER), F32)],
        compiler_params=_params(("parallel", "arbitrary")),
        name="ssd",
    )(z, xbc, xbc, dt, conv_w, conv_b.reshape(1, SSD_XBC), pad(dt_bias), pad(a_log), d_exp,
      norm_g.reshape(1, SSD_INNER), expand)


def _cmp_kernel(kvc_ref, pea_ref, peb_ref, w1a_ref, w1b_ref, w2_ref, kn_ref, o_ref):
    n_chunk = kvc_ref.shape[0]
    per = NSA_CMP_STRIDE
    row = lax.broadcasted_iota(jnp.int32, (n_chunk, LANES), 0)
    for g in range(NSA_KV_GROUPS):
        x = jnp.concatenate([kvc_ref[:, (NSA_KV_GROUPS * l + g) * LANES:(NSA_KV_GROUPS * l + g + 1) * LANES]
                             for l in range(per)], axis=1).astype(F32)
        a = _dot((x + pea_ref[...]).astype(BF16), w1a_ref[...])
        b = _dot((x + peb_ref[...]).astype(BF16), w1b_ref[...])
        hid = _silu(a + pltpu.roll(b, n_chunk - 1, axis=0))
        cmp = _dot(hid.astype(BF16), w2_ref[...])
        cmp = _head_rms(cmp, kn_ref[...], NSA_HEAD_DIM)
        o_ref[g] = jnp.where(row < n_chunk - 1, cmp, 0.0).astype(o_ref.dtype)


def _cmp(kvc, nsa_cmp_pe, nsa_cmp_w1, nsa_cmp_w2, k_norm0):
    B, S, _ = kvc.shape
    n_chunk = S // NSA_CMP_STRIDE
    per = NSA_CMP_STRIDE
    dk = NSA_HEAD_DIM
    kvc_r = kvc.reshape(B, n_chunk, per * NSA_KV_GROUPS * LANES)
    pe = jnp.concatenate([nsa_cmp_pe[0], nsa_cmp_pe[1]], axis=1)
    pea = pe[:per].reshape(1, per * LANES)
    peb = pe[per:].reshape(1, per * LANES)
    w1 = nsa_cmp_w1.reshape(2, NSA_CMP_LEN, dk, NSA_CMP_HIDDEN)
    zero = jnp.zeros((NSA_CMP_LEN, dk, NSA_CMP_HIDDEN), F32)
    w1bd = jnp.concatenate([jnp.concatenate([w1[0], zero], axis=2), jnp.concatenate([zero, w1[1]], axis=2)], axis=1)
    w1a = w1bd[:per].reshape(per * LANES, 2 * NSA_CMP_HIDDEN).astype(BF16)
    w1b = w1bd[per:].reshape(per * LANES, 2 * NSA_CMP_HIDDEN).astype(BF16)
    z2 = jnp.zeros((NSA_CMP_HIDDEN, dk), F32)
    w2bd = jnp.concatenate([jnp.concatenate([nsa_cmp_w2[0], z2], axis=1),
                            jnp.concatenate([z2, nsa_cmp_w2[1]], axis=1)], axis=0).astype(BF16)
    kn = jnp.concatenate([k_norm0, jnp.ones((LANES - dk,), F32)]).reshape(1, LANES)
    return pl.pallas_call(
        _cmp_kernel,
        grid=(B,),
        in_specs=[pl.BlockSpec((None, n_chunk, per * NSA_KV_GROUPS * LANES), lambda b: (b, 0, 0)),
                  _const_spec(pea.shape), _const_spec(peb.shape), _const_spec(w1a.shape), _const_spec(w1b.shape),
                  _const_spec(w2bd.shape), _const_spec(kn.shape)],
        out_specs=pl.BlockSpec((None, NSA_KV_GROUPS, n_chunk, LANES), lambda b: (b, 0, 0, 0)),
        out_shape=jax.ShapeDtypeStruct((B, NSA_KV_GROUPS, n_chunk, LANES), BF16),
        compiler_params=_params(("parallel",)),
        name="nsa_cmp",
    )(kvc_r, pea, peb, w1a, w1b, w2bd, kn)


def _top_blocks(imp, jt):
    n_slc, TQ = imp.shape
    blk_q = lax.broadcasted_iota(jnp.int32, (n_slc, TQ), 0)
    imp = jnp.where(blk_q == 0, BIG, imp)
    imp = jnp.where(blk_q == jt, BIG, imp)
    imp = jnp.where(blk_q == jt - 1, BIG, imp)
    imp = jnp.where(blk_q > jt, NEG, imp)
    SUB = 8
    row8 = lax.broadcasted_iota(jnp.int32, (SUB, 1), 0)
    groups = [imp[k * SUB:(k + 1) * SUB] for k in range(n_slc // SUB)]
    ranks = [jnp.zeros((SUB, TQ), F32) for _ in groups]
    for i in range(n_slc):
        vi = jnp.broadcast_to(imp[i:i + 1, :], (SUB, TQ))
        for k, grp in enumerate(groups):
            if k * SUB > i:
                ahead = jnp.where(vi >= grp, 1.0, 0.0)
            elif k * SUB + SUB - 1 < i:
                ahead = jnp.where(vi > grp, 1.0, 0.0)
            else:
                ahead = jnp.where(row8 + k * SUB > i, jnp.where(vi >= grp, 1.0, 0.0), jnp.where(vi > grp, 1.0, 0.0))
            ranks[k] = ranks[k] + ahead
    rank = jnp.concatenate(ranks, axis=0)
    return jnp.where(rank < NSA_TOP_N, jnp.where(blk_q <= jt, 1.0, 0.0), 0.0)


def _nsa_kernel(*refs, seq):
    n_slc = seq // NSA_SLC_BLOCK
    live_step = n_slc // NSA_STEP_VARIANTS
    variant = ((pl.program_id(1) + 1) * NSA_TQ - 1) // NSA_SLC_BLOCK // live_step
    for v in range(NSA_STEP_VARIANTS):
        pl.when(variant == v)(functools.partial(_nsa_step, *refs, seq=seq, n_live=(v + 1) * live_step))


def _nsa_step(q_ref, kvc_ref, kv_ref, gate_ref, ov_ref, o_ref, k2c, vct, k2s, vst, k2w, vwt, s_a, s_b, m_st, acc_st,
              sw_buf, *, seq, n_live):
    TQ, TK, W = NSA_TQ, NSA_TK, NSA_WINDOW
    G, HG, DK = NSA_KV_GROUPS, NSA_HG, NSA_HEAD_DIM
    PAIR = 2
    n_cmp_pad = kvc_ref.shape[1]
    n_slc = seq // NSA_SLC_BLOCK
    t0 = pl.program_id(1) * TQ
    slopes = [[2.0 ** -(g * HG + h + 1) for h in range(HG)] for g in range(G)]

    @pl.when(pl.program_id(1) == 0)
    def _build_keys():
        lane = lax.broadcasted_iota(jnp.int32, (LANES, LANES), 1)
        row = lax.broadcasted_iota(jnp.int32, (LANES, LANES), 0)

        def with_pos(kv, pos):
            rel = lane - DK
            feat = jnp.where(rel % 2 == 0, pos // NSA_SLC_BLOCK, pos % NSA_SLC_BLOCK)
            feat = jnp.where(rel < 2 * len(LOG2E_TERMS), feat, 0).astype(F32)
            return jnp.where(rel < 0, kv, feat).astype(BF16)

        ones_row = jnp.where(lax.broadcasted_iota(jnp.int32, (BF16_SUBLANES, LANES), 0) == 0, 1.0, 0.0)

        def values_t(kv):
            return jnp.concatenate([kv.T[DK:], ones_row], axis=0).astype(BF16)

        for g in range(G):
            for c in range(n_cmp_pad // LANES):
                rows = slice(c * LANES, (c + 1) * LANES)
                kv = kvc_ref[g, rows, :].astype(F32)
                k2c[g, rows, :] = with_pos(kv, (row + c * LANES) * NSA_CMP_STRIDE + (NSA_CMP_LEN - 1))
                vct[g, :, rows] = values_t(kv)

        def chunk(c, carry):
            r0 = pl.multiple_of(c * LANES, LANES)
            pos = row + r0
            one_hot = jnp.where(lane == pos // NSA_SLC_BLOCK, 1.0, 0.0).astype(BF16)
            for g in range(G):
                kv = kv_ref[pl.ds(r0, LANES), 2 * g * LANES:(2 * g + 1) * LANES].astype(F32)
                k2s[g, pl.ds(r0, LANES), 0:LANES] = with_pos(kv, pos)
                k2s[g, pl.ds(r0, LANES), LANES:2 * LANES] = one_hot
                vst[g, :, pl.ds(r0, LANES)] = values_t(kv)
                kvw = kv_ref[pl.ds(r0, LANES), (2 * g + 1) * LANES:(2 * g + 2) * LANES].astype(F32)
                k2w[g, pl.ds(r0, LANES), :] = with_pos(kvw, pos)
                vwt[g, :, pl.ds(r0, LANES)] = values_t(kvw)
            return carry

        lax.fori_loop(0, seq // LANES, chunk, 0)

    rel_q = lax.broadcasted_iota(jnp.int32, (1, LANES), 1) - DK
    lane_lo = lax.broadcasted_iota(jnp.int32, (TQ, LANES), 1) < DK
    q1 = [[None] * HG for _ in range(G)]
    for g in range(G):
        for h in range(HG):
            feat = jnp.zeros((1, LANES), F32)
            for i, term in enumerate(LOG2E_TERMS):
                feat = jnp.where(rel_q == 2 * i, NSA_SLC_BLOCK * slopes[g][h] * term, feat)
                feat = jnp.where(rel_q == 2 * i + 1, slopes[g][h] * term, feat)
            head = g * HG + h
            qpair = q_ref[:, (head // 2) * LANES:(head // 2 + 1) * LANES].astype(F32)
            if head % 2:
                qpair = pltpu.roll(qpair, DK, axis=1)
            q1[g][h] = (jnp.where(lane_lo, qpair, 0.0) + feat).astype(BF16)
    chains = [(g, hp) for g in range(G) for hp in range(HG // PAIR)]
    q1c = {(g, hp): jnp.concatenate(q1[g][hp * PAIR:(hp + 1) * PAIR], axis=0) for g, hp in chains}
    tq = t0 + lax.broadcasted_iota(jnp.int32, (1, TQ), 1)
    jt = tq // NSA_SLC_BLOCK

    def tile_pair(mask):
        return jnp.concatenate([mask] * PAIR, axis=1)

    n_cmp_pad = min(n_cmp_pad, -(-n_live * (NSA_SLC_BLOCK // NSA_CMP_STRIDE) // LANES) * LANES)
    cend = lax.broadcasted_iota(jnp.int32, (n_cmp_pad, 1), 0) * NSA_CMP_STRIDE + (NSA_CMP_LEN - 1)
    valid = (tq - cend) >= 0
    o_cmp = {}
    psum = [jnp.zeros((n_cmp_pad, TQ), F32) for _ in range(G)]
    sc_all = [_dot_nt(k2c[g, 0:n_cmp_pad], q1c[g, hp]) for g, hp in chains]
    band = W + TQ
    w0 = pl.multiple_of(jnp.maximum(t0 - W, 0), TQ)
    dw = tq - (w0 + lax.broadcasted_iota(jnp.int32, (band, 1), 0))
    in_win = tile_pair(jnp.where(dw >= 0, jnp.where(dw < W, 0.0, NEG), NEG))
    for (g, hp), sct in zip(chains, sc_all):
        p_list = []
        for j in range(PAIR):
            s = jnp.where(valid, sct[:, j * TQ:(j + 1) * TQ], NEG)
            m = jnp.max(s, axis=0, keepdims=True)
            p_list.append(jnp.where(valid, jnp.exp2(s - m), 0.0))
        pv = _dot(vct[g, :, 0:n_cmp_pad], jnp.concatenate(p_list, axis=1).astype(BF16))
        l = pv[DK:DK + 1]
        inv = 1.0 / jnp.where(l > 0.0, l, 1.0)
        o_cmp[g, hp] = pv[0:DK] * inv
        for j in range(PAIR):
            psum[g] = psum[g] + p_list[j] * inv[:, j * TQ:(j + 1) * TQ]
    for ci, (g, hp) in enumerate(chains):
        sw_buf[ci] = _dot_nt(k2w[g, pl.ds(w0, band), :], q1c[g, hp])
    q2c = {}
    for g in range(G):
        imp = sum(_dot(ov_ref[:, 0:n_cmp_pad], part) for part in _split3(psum[g]))
        sel = _top_blocks(imp[0:n_live], jt)
        sel = jnp.concatenate([sel, jnp.zeros((LANES - n_live, TQ), F32)], axis=0)
        sel_neg = jnp.where(sel.T > 0.5, 0.0, -(2.0 ** 100)).astype(BF16)
        for hp in range(HG // PAIR):
            q2c[g, hp] = jnp.concatenate([jnp.concatenate([q1[g][hp * PAIR + j], sel_neg], axis=1)
                                          for j in range(PAIR)], axis=0)

    def slc_scores(kt, buf):
        k0 = pl.multiple_of(kt * TK, TK)
        for ci, (g, hp) in enumerate(chains):
            buf[ci] = _dot_nt(k2s[g, pl.ds(k0, TK), :], q2c[g, hp])

    def slc_softmax(kt, buf, masked):
        k0 = pl.multiple_of(kt * TK, TK)
        if masked:
            kpos = k0 + lax.broadcasted_iota(jnp.int32, (TK, 1), 0)
            causal_add = tile_pair(jnp.where(kpos <= tq, 0.0, NEG))
        for ci, (g, hp) in enumerate(chains):
            s = buf[ci]
            if masked:
                s = s + causal_add
            m = m_st[ci]
            m_new = jnp.maximum(m, jnp.max(s, axis=0, keepdims=True))
            p = jnp.exp2(s - m_new)
            acc_st[ci] = jnp.exp2(m - m_new) * acc_st[ci] + _dot(vst[g, :, pl.ds(k0, TK)], p.astype(BF16))
            m_st[ci] = m_new

    n_past = t0 // TK
    m_st[...] = jnp.full(m_st.shape, NEG, F32)
    acc_st[...] = jnp.zeros(acc_st.shape, F32)
    slc_scores(0, s_a)

    def slc_pair(j, carry):
        slc_scores(2 * j + 1, s_b)
        slc_softmax(2 * j, s_a, False)
        slc_scores(2 * j + 2, s_a)
        slc_softmax(2 * j + 1, s_b, False)
        return carry

    o_win = {}
    for ci, (g, hp) in enumerate(chains):
        sw = sw_buf[ci] + in_win
        pw = jnp.exp2(sw - jnp.max(sw, axis=0, keepdims=True))
        pv = _dot(vwt[g, :, pl.ds(w0, band)], pw.astype(BF16))
        o_win[g, hp] = pv[0:DK] / pv[DK:DK + 1]

    lax.fori_loop(0, n_past // 2, slc_pair, 0)

    @pl.when(n_past % 2 == 1)
    def _odd_tail():
        slc_scores(n_past, s_b)
        slc_softmax(n_past - 1, s_a, False)
        slc_softmax(n_past, s_b, True)

    @pl.when(n_past % 2 == 0)
    def _even_tail():
        slc_softmax(n_past, s_a, True)

    o_slc = {ch: acc_st[ci, 0:DK] / acc_st[ci, DK:DK + 1] for ci, ch in enumerate(chains)}

    comb = []
    gt = jax.nn.sigmoid(gate_ref[...]).T
    for g in range(G):
        for h in range(HG):
            ch = (g, h // PAIR)
            cols = slice((h % PAIR) * TQ, (h % PAIR + 1) * TQ)
            gate = [gt[r:r + 1] for r in (SSD_HEADS + (g * 3 + br) * HG + h for br in range(3))]
            comb.append(gate[0] * o_cmp[ch][:, cols] + gate[1] * o_slc[ch][:, cols]
                        + gate[2] * o_win[ch][:, cols])
    o_ref[...] = jnp.concatenate(comb, axis=0).T.astype(o_ref.dtype)


def _nsa(q, kvcmp, kv, gates, B, S):
    TQ = NSA_TQ
    G = NSA_KV_GROUPS
    assert S % (NSA_CMP_STRIDE * LANES) == 0 and S % NSA_TK == 0 and NSA_TK % TQ == 0 and S >= NSA_WINDOW + TQ
    n_cmp_pad = S // NSA_CMP_STRIDE
    n_slc = S // NSA_SLC_BLOCK
    cs = jnp.arange(n_cmp_pad) * NSA_CMP_STRIDE
    ce = cs + NSA_CMP_LEN - 1
    ss = jnp.arange(LANES) * NSA_SLC_BLOCK
    overlap = jnp.clip(jnp.minimum(ce[None, :], ss[:, None] + NSA_SLC_BLOCK - 1)
                       - jnp.maximum(cs[None, :], ss[:, None]) + 1, 0).astype(F32) / NSA_CMP_LEN
    valid_pair = (jnp.arange(LANES)[:, None] < n_slc) & (jnp.arange(n_cmp_pad)[None, :] < n_cmp_pad - 1)
    ov_t = jnp.where(valid_pair, overlap, 0.0).astype(BF16)
    q3 = q.reshape(B, S, NSA_INNER)
    kv3 = kv.reshape(B, S, G * 2 * LANES)
    g3 = gates.reshape(B, S, LANES)
    vt_rows = NSA_HEAD_DIM + BF16_SUBLANES
    n_chains = NSA_HEADS // 2
    return pl.pallas_call(
        functools.partial(_nsa_kernel, seq=S),
        grid=(B, S // TQ),
        in_specs=[pl.BlockSpec((None, TQ, NSA_INNER), lambda b, i: (b, i, 0)),
                  pl.BlockSpec((None, G, n_cmp_pad, LANES), lambda b, i: (b, 0, 0, 0)),
                  pl.BlockSpec((None, S, G * 2 * LANES), lambda b, i: (b, 0, 0)),
                  pl.BlockSpec((None, TQ, LANES), lambda b, i: (b, i, 0)),
                  _const_spec(ov_t.shape)],
        out_specs=pl.BlockSpec((None, TQ, NSA_INNER), lambda b, i: (b, i, 0)),
        out_shape=jax.ShapeDtypeStruct((B, S, NSA_INNER), BF16),
        scratch_shapes=[pltpu.VMEM((G, n_cmp_pad, LANES), BF16), pltpu.VMEM((G, vt_rows, n_cmp_pad), BF16),
                        pltpu.VMEM((G, S, 2 * LANES), BF16), pltpu.VMEM((G, vt_rows, S), BF16),
                        pltpu.VMEM((G, S, LANES), BF16), pltpu.VMEM((G, vt_rows, S), BF16),
                        pltpu.VMEM((n_chains, NSA_TK, 2 * TQ), F32), pltpu.VMEM((n_chains, NSA_TK, 2 * TQ), F32),
                        pltpu.VMEM((n_chains, 1, 2 * TQ), F32), pltpu.VMEM((n_chains, vt_rows, 2 * TQ), F32),
                        pltpu.VMEM((n_chains, NSA_WINDOW + TQ, 2 * TQ), F32)],
        compiler_params=_params(("parallel", "arbitrary")),
        name="nsa_attn",
    )(q3, kvcmp, kv3, g3, ov_t)


def _mem_kv_kernel(mem_ref, g_ref, w_ref, kn_ref, k_o, v_o):
    mn = _rms_rows(mem_ref[...], g_ref[...]).astype(BF16)
    kv = _dot(mn, w_ref[...])
    for h in range(MEM_HEADS):
        sl = slice(h * LANES, (h + 1) * LANES)
        k_o[:, sl] = _head_rms(kv[:, sl], kn_ref[...], MEM_HEAD_DIM).astype(k_o.dtype)
    v_o[...] = kv[:, MEM_INNER:].astype(v_o.dtype)


def _mem_kv(mem, mem_norm, w_mem_kv, mem_k_norm):
    B, M, _ = mem.shape
    spec_o = pl.BlockSpec((None, M, MEM_INNER), lambda b: (b, 0, 0))
    return pl.pallas_call(
        _mem_kv_kernel,
        grid=(B,),
        in_specs=[pl.BlockSpec((None, M, D_MODEL), lambda b: (b, 0, 0)), _const_spec((1, D_MODEL)),
                  _const_spec((D_MODEL, 2 * MEM_INNER)), _const_spec((1, LANES))],
        out_specs=[spec_o, spec_o],
        out_shape=[jax.ShapeDtypeStruct((B, M, MEM_INNER), BF16)] * 2,
        compiler_params=_params(("parallel",)),
        name="mem_kv",
    )(mem, mem_norm.reshape(1, D_MODEL), w_mem_kv.astype(BF16), mem_k_norm.reshape(1, LANES))


def _merge_kernel(x_ref, ssd_ref, nsa_ref, qm_ref, gm_ref, km_ref, vm_ref, wso, wno, wmo, wout, h_o):
    heads = [slice(h * LANES, (h + 1) * LANES) for h in range(MEM_HEADS)]
    scores = [_dot_nt(qm_ref[:, sl], km_ref[:, sl]) for sl in heads]
    y_ssd = _dot(ssd_ref[...], wso[...])
    y_nsa = _dot(nsa_ref[...], wno[...])
    o_mem = []
    for sl, s in zip(heads, scores):
        p = jnp.exp2(s - jnp.max(s, axis=-1, keepdims=True))
        o = _dot(p.astype(BF16), vm_ref[:, sl]) / jnp.sum(p, axis=-1, keepdims=True)
        o_mem.append(o.astype(BF16))
    y_mem = _dot(jnp.concatenate(o_mem, axis=1), wmo[...])
    gate = lambda j: jax.nn.sigmoid(gm_ref[:, j * D_MODEL:(j + 1) * D_MODEL].astype(F32))
    mixed = gate(0) * y_ssd + gate(1) * y_nsa + gate(2) * y_mem
    h_o[...] = x_ref[...] + _dot(mixed.astype(BF16), wout[...])


def _merge(x2, ssd_n, nsa_o, q_mem, g_merge, k_mem, v_mem, w_ssd_o, w_nsa_o, w_mem_o, w_out, S):
    T = x2.shape[0]
    TM = ROW_TILE
    M = k_mem.shape[1]
    per_batch = S // TM
    row = lambda w: pl.BlockSpec((TM, w), lambda i: (i, 0))
    mem_spec = pl.BlockSpec((None, M, MEM_INNER), lambda i: (i // per_batch, 0, 0))
    weights = [w.astype(BF16) for w in (w_ssd_o, w_nsa_o, w_mem_o, w_out)]
    return pl.pallas_call(
        _merge_kernel,
        grid=(T // TM,),
        in_specs=[row(D_MODEL), row(SSD_INNER), row(NSA_INNER), row(MEM_INNER), row(N_BRANCH * D_MODEL),
                  mem_spec, mem_spec] + [_const_spec(w.shape) for w in weights],
        out_specs=row(D_MODEL),
        out_shape=jax.ShapeDtypeStruct((T, D_MODEL), F32),
        compiler_params=_params(("parallel",)),
        name="merge",
    )(x2, ssd_n, nsa_o, q_mem, g_merge, k_mem, v_mem, *weights)


def _ffn_kernel(h_ref, halo_ref, g_ref, wup, cw_ref, cb_ref, wdown, o_ref, *, per_batch):
    TM = h_ref.shape[0]
    HALO = halo_ref.shape[0]
    CW = FFN_CHUNK
    first = pl.program_id(0) % per_batch == 0
    h = h_ref[...]
    halo_n = _rms_rows(halo_ref[...], g_ref[...]) * jnp.where(first, 0.0, 1.0)
    hn = jnp.concatenate([halo_n, _rms_rows(h, g_ref[...])], axis=0).astype(BF16)
    n_chunks = FFN_HIDDEN // CW

    def up_proj(c):
        return [_dot(hn, wup[:, half * FFN_HIDDEN + c * CW:half * FFN_HIDDEN + (c + 1) * CW]) for half in range(2)]

    acc = jnp.zeros((TM, D_MODEL), F32)
    ext = up_proj(0)
    pending = []
    for c in range(n_chunks):
        ext_next = up_proj(c + 1) if c + 1 < n_chunks else None
        conv = []
        for half in range(2):
            col = half * FFN_HIDDEN + c * CW
            u = jnp.broadcast_to(cb_ref[:, col:col + CW], (TM, CW))
            for k in range(FFN_CONV):
                start = HALO - (FFN_CONV - 1) + k
                u = u + cw_ref[k:k + 1, col:col + CW] * ext[half][start:start + TM]
            conv.append(u)
        pending.append((_silu(conv[0]) * conv[1]).astype(BF16))
        if len(pending) == FFN_DOWN_GROUP or c + 1 == n_chunks:
            lo = (c + 1 - len(pending)) * CW
            acc = acc + _dot(jnp.concatenate(pending, axis=1), wdown[lo:(c + 1) * CW, :])
            pending = []
        ext = ext_next
    o_ref[...] = h + acc


def _ffn(h2, norm_ffn, w_ffn_up, ffn_conv_w, ffn_conv_b, w_ffn_down, S):
    T = h2.shape[0]
    TM = ROW_TILE
    HALO = BF16_SUBLANES
    per_batch = S // TM
    return pl.pallas_call(
        functools.partial(_ffn_kernel, per_batch=per_batch),
        grid=(T // TM,),
        in_specs=[pl.BlockSpec((TM, D_MODEL), lambda i: (i, 0)),
                  pl.BlockSpec((HALO, D_MODEL), lambda i: (jnp.maximum(i * (TM // HALO) - 1, 0), 0)),
                  _const_spec((1, D_MODEL)), _const_spec((D_MODEL, 2 * FFN_HIDDEN)),
                  _const_spec((FFN_CONV, 2 * FFN_HIDDEN)), _const_spec((1, 2 * FFN_HIDDEN)),
                  _const_spec((FFN_HIDDEN, D_MODEL))],
        out_specs=pl.BlockSpec((TM, D_MODEL), lambda i: (i, 0)),
        out_shape=jax.ShapeDtypeStruct((T, D_MODEL), F32),
        compiler_params=_params(("parallel",)),
        name="ffn",
    )(h2, h2, norm_ffn.reshape(1, D_MODEL), w_ffn_up.astype(BF16), ffn_conv_w,
      ffn_conv_b.reshape(1, 2 * FFN_HIDDEN), w_ffn_down.astype(BF16))


def _layer(x, mem, norm_mix, w_in, ssd_conv_w, ssd_conv_b, ssd_dt_bias, ssd_a_log, ssd_d, ssd_norm, w_ssd_o,
           nsa_q_norm, nsa_k_norm, nsa_cmp_pe, nsa_cmp_w1, nsa_cmp_w2, w_nsa_o, mem_norm, w_mem_kv,
           mem_q_norm, mem_k_norm, w_mem_o, w_out, norm_ffn, w_ffn_up, ffn_conv_w, ffn_conv_b, w_ffn_down):
    B, S, D = x.shape
    x2 = x.reshape(B * S, D)
    z, xbc, small, q, kvc, kv, qm, gm = _in_proj(x2, norm_mix, w_in, nsa_q_norm, nsa_k_norm, mem_q_norm)
    ssd_n = _ssd(z.reshape(B, S, -1), xbc.reshape(B, S, -1), small.reshape(B, S, -1), ssd_conv_w, ssd_conv_b,
                 ssd_dt_bias, ssd_a_log, ssd_d, ssd_norm)
    kvcmp = _cmp(kvc.reshape(B, S, -1), nsa_cmp_pe, nsa_cmp_w1, nsa_cmp_w2, nsa_k_norm[0])
    nsa_o = _nsa(q, kvcmp, kv, small, B, S)
    k_mem, v_mem = _mem_kv(mem, mem_norm, w_mem_kv, mem_k_norm)
    h2 = _merge(x2, ssd_n.reshape(B * S, -1), nsa_o.reshape(B * S, -1), qm, gm, k_mem, v_mem,
                w_ssd_o, w_nsa_o, w_mem_o, w_out, S)
    out = _ffn(h2, norm_ffn, w_ffn_up, ffn_conv_w, ffn_conv_b, w_ffn_down, S)
    return out.reshape(B, S, D)


def kernel(x, mem, norm_mix, w_in, ssd_conv_w, ssd_conv_b, ssd_dt_bias, ssd_a_log, ssd_d, ssd_norm, w_ssd_o, nsa_q_norm, nsa_k_norm, nsa_cmp_pe, nsa_cmp_w1, nsa_cmp_w2, w_nsa_o, mem_norm, w_mem_kv, mem_q_norm, mem_k_norm, w_mem_o, w_out, norm_ffn, w_ffn_up, ffn_conv_w, ffn_conv_b, w_ffn_down):
    h = x
    for i in range(norm_mix.shape[0]):
        h = _layer(h, mem, norm_mix[i], w_in[i], ssd_conv_w[i], ssd_conv_b[i], ssd_dt_bias[i], ssd_a_log[i],
                   ssd_d[i], ssd_norm[i], w_ssd_o[i], nsa_q_norm[i], nsa_k_norm[i], nsa_cmp_pe[i],
                   nsa_cmp_w1[i], nsa_cmp_w2[i], w_nsa_o[i], mem_norm[i], w_mem_kv[i], mem_q_norm[i],
                   mem_k_norm[i], w_mem_o[i], w_out[i], norm_ffn[i], w_ffn_up[i], ffn_conv_w[i],
                   ffn_conv_b[i], w_ffn_down[i])
    return h
```
